```python
import math
import jax, jax.numpy as jnp
from jax import lax
import numpy as np

D_MODEL = 1024
BATCH = 2
SEQ = 8192
DEPTH = 1
DEC_BATCH = 128
DEC_SEQ = 8
PAST_LEN = 16384
PAGE_SIZE = 128

N_META = 16
Q_BLOCK = 128
EPS = 1e-6
MLA_HEADS = 8
MLA_NOPE = 64
MLA_ROPE = 32
MLA_QK = MLA_NOPE + MLA_ROPE
MLA_V = 64
MLA_Q_LORA = 256
MLA_KV_LORA = 128
MLA_SCALE = 1.0 / math.sqrt(MLA_QK)
ROPE_BASE = 10000.0
DIFF_HEADS = 8
DIFF_KV_HEADS = 4
DIFF_REP = DIFF_HEADS // DIFF_KV_HEADS
DIFF_HEAD_DIM = 64
DIFF_V = 2 * DIFF_HEAD_DIM
DIFF_SCALE = 1.0 / math.sqrt(DIFF_HEAD_DIM)
N_EXPERTS = 32
TOP_K = 4
D_FF = D_MODEL
SWIGLU_LIMIT = 7.0
SWIGLU_ALPHA = 1.702
IN_WIDTHS = (MLA_Q_LORA, MLA_KV_LORA, MLA_ROPE, DIFF_HEADS * 2 * DIFF_HEAD_DIM,
             DIFF_KV_HEADS * 2 * DIFF_HEAD_DIM, DIFF_KV_HEADS * DIFF_V, D_MODEL, D_MODEL)
IN_SPLITS = tuple(sum(IN_WIDTHS[:i + 1]) for i in range(len(IN_WIDTHS) - 1))
D_IN = sum(IN_WIDTHS)

kernel_name = 'hybrid_mla_diffattn_moe_decode_step'


def rmsnorm(x, g):
    xf = x.astype(jnp.float32)
    y = xf * lax.rsqrt(jnp.mean(xf * xf, axis=-1, keepdims=True) + EPS)
    return (y * g.astype(jnp.float32)).astype(x.dtype)


def rope_tables(pos):
    inv = 1.0 / (ROPE_BASE ** (jnp.arange(0, MLA_ROPE, 2, dtype=jnp.float32) / MLA_ROPE))
    ang = pos.astype(jnp.float32)[:, None] * inv[None, :]
    return jnp.cos(ang), jnp.sin(ang)


def apply_rope(x, cos, sin):
    cos = cos.astype(x.dtype)
    sin = sin.astype(x.dtype)
    x1, x2 = jnp.split(x, 2, axis=-1)
    return jnp.concatenate([x1 * cos - x2 * sin, x2 * cos + x1 * sin], axis=-1)


def alibi_slopes():
    return jnp.exp2(-8.0 * jnp.arange(1, DIFF_HEADS + 1, dtype=jnp.float32) / DIFF_HEADS)


def mixer_inputs(x, pos, lp):
    b, t = x.shape[:2]
    z = rmsnorm(x, lp['g_attn']) @ lp['w_in']
    q_lat, kv_lat, k_pe, q_d, k_d, v_d, gate_a, gate_b = jnp.split(z, IN_SPLITS, axis=-1)
    cos, sin = rope_tables(pos)
    q = jnp.einsum('btc,chd->bthd', rmsnorm(q_lat, lp['g_q_lat']), lp['w_uq'])
    q = jnp.concatenate([q[..., :MLA_NOPE],
                         apply_rope(q[..., MLA_NOPE:], cos[None, :, None], sin[None, :, None])], axis=-1)
    q_a = rmsnorm(q, lp['g_mla_q']) * MLA_SCALE
    lat = rmsnorm(kv_lat, lp['g_kv_lat'])
    kpe = apply_rope(k_pe, cos[None], sin[None])
    q_d = rmsnorm(q_d.reshape(b, t, DIFF_HEADS, 2, DIFF_HEAD_DIM), lp['g_diff_q']) * DIFF_SCALE
    k_d = rmsnorm(k_d.reshape(b, t, DIFF_KV_HEADS, 2, DIFF_HEAD_DIM), lp['g_diff_k'])
    v_d = v_d.reshape(b, t, DIFF_KV_HEADS, DIFF_V)
    return q_a, lat, kpe, q_d, k_d, v_d, gate_a, gate_b


def mla_keys_values(lat, kpe, lp):
    k_nope = jnp.einsum('btc,chd->bthd', lat, lp['w_uk'])
    k_rope = jnp.broadcast_to(kpe[:, :, None, :], k_nope.shape[:3] + (MLA_ROPE,))
    k = rmsnorm(jnp.concatenate([k_nope, k_rope], axis=-1), lp['g_mla_k'])
    v = jnp.einsum('btc,chd->bthd', lat, lp['w_uv'])
    return k, v


def mla_attend(q, qpos, k, v, kpos):
    s = jnp.einsum('bqhd,bkhd->bhqk', q, k, preferred_element_type=jnp.float32)
    s = jnp.where(kpos[None, :] <= qpos[:, None], s, -jnp.inf)
    p = jax.nn.softmax(s, axis=-1).astype(v.dtype)
    return jnp.einsum('bhqk,bkhd->bqhd', p, v)


def diff_attend(q, qpos, k, v, kpos, lam):
    b, nq = q.shape[:2]
    qg = q.reshape(b, nq, DIFF_KV_HEADS, DIFF_REP, 2, DIFF_HEAD_DIM)
    s = jnp.einsum('bqgrcd,bkgcd->bgrcqk', qg, k, preferred_element_type=jnp.float32)
    dist = (qpos[:, None] - kpos[None, :]).astype(jnp.float32)
    slopes = alibi_slopes().reshape(DIFF_KV_HEADS, DIFF_REP, 1, 1, 1)
    s = jnp.where(dist >= 0, s - slopes * dist, -jnp.inf)
    p = jax.nn.softmax(s, axis=-1)
    a = (p[:, :, :, 0] - lam * p[:, :, :, 1]).astype(v.dtype)
    o = jnp.einsum('bgrqk,bkgv->bqgrv', a, v)
    return o.reshape(b, nq, DIFF_HEADS, DIFF_V)


def sweep_query_blocks(attend, q, qpos):
    b, t = q.shape[:2]
    n_blk = (t - N_META) // Q_BLOCK
    head = attend(q[:, :N_META], qpos[:N_META])
    qb = q[:, N_META:].reshape((b, n_blk, Q_BLOCK) + q.shape[2:]).swapaxes(0, 1)
    pb = qpos[N_META:].reshape(n_blk, Q_BLOCK)
    ob = lax.map(lambda qp: attend(qp[0], qp[1]), (qb, pb))
    ob = ob.swapaxes(0, 1).reshape((b, t - N_META) + ob.shape[3:])
    return jnp.concatenate([head, ob], axis=1)


def merge_branches(x, o_a, o_d, gate_a, gate_b, lp, lambda_init):
    b, t = x.shape[:2]
    y_a = o_a.reshape(b, t, MLA_HEADS * MLA_V) @ lp['w_branch_a']
    o_d = rmsnorm(o_d, lp['g_subln']) * (1.0 - lambda_init)
    y_b = o_d.reshape(b, t, DIFF_HEADS * DIFF_V) @ lp['w_branch_b']
    m = jax.nn.sigmoid(gate_a) * y_a + jax.nn.sigmoid(gate_b) * y_b
    return x + m @ lp['w_o']


def moe_ffn(x, lp):
    shape = x.shape
    h = rmsnorm(x, lp['g_ffn']).reshape(-1, shape[-1])
    logits = jnp.dot(h, lp['w_router'], preferred_element_type=jnp.float32) + lp['b_router'].astype(jnp.float32)
    top_v, top_i = lax.top_k(logits, TOP_K)
    top_w = jax.nn.softmax(top_v, axis=-1)
    gates = jnp.einsum('nk,nke->en', top_w, jax.nn.one_hot(top_i, N_EXPERTS, dtype=jnp.float32))

    def expert(acc, params):
        w_gu, b_gu, w_dn, b_dn, g_e = params
        gu = h @ w_gu + b_gu
        glu_in = jnp.minimum(gu[:, 0::2], SWIGLU_LIMIT)
        lin = jnp.clip(gu[:, 1::2], -SWIGLU_LIMIT, SWIGLU_LIMIT)
        y = (glu_in * jax.nn.sigmoid(SWIGLU_ALPHA * glu_in) * (lin + 1.0)) @ w_dn + b_dn
        return acc + g_e[:, None] * y.astype(jnp.float32), None

    acc, _ = lax.scan(expert, jnp.zeros(h.shape, jnp.float32),
                      (lp['w_gate_up'], lp['b_gate_up'], lp['w_down'], lp['b_down'], gates))
    return x + acc.astype(x.dtype).reshape(shape)


def setup_inputs(seed: int = 0) -> dict:
    key = jax.random.key(seed)
    ks = jax.random.split(key, 40)
    f32 = jnp.float32
    n_pages = PAST_LEN // PAGE_SIZE
    n_used = DEC_BATCH * n_pages
    n_pool = n_used + n_used // 4

    def nrm(i, shape):
        return jax.random.normal(ks[i], shape, f32)

    def gain(i, shape):
        return 1.0 + 0.02 * nrm(i, shape)

    page_table = jax.random.permutation(ks[39], n_pool)[:n_used].reshape(DEC_BATCH, n_pages).astype(jnp.int32)
    return {
        'x_prompt': nrm(0, (BATCH, SEQ, D_MODEL)),
        'x_sample': nrm(1, (DEC_BATCH, DEC_SEQ, D_MODEL)),
        'cache_mla_latent': nrm(2, (DEPTH, n_pool, PAGE_SIZE, MLA_KV_LORA)),
        'cache_mla_krope': nrm(3, (DEPTH, n_pool, PAGE_SIZE, MLA_ROPE)),
        'cache_diff_k': nrm(4, (DEPTH, n_pool, PAGE_SIZE, DIFF_KV_HEADS, 2, DIFF_HEAD_DIM)),
        'cache_diff_v': nrm(5, (DEPTH, n_pool, PAGE_SIZE, DIFF_KV_HEADS, DIFF_V)),
        'page_table': page_table,
        'meta_tokens': nrm(6, (N_META, D_MODEL)),
        'g_attn': gain(7, (DEPTH, D_MODEL)),
        'w_in': nrm(8, (DEPTH, D_MODEL, D_IN)) * D_MODEL ** -0.5,
        'g_q_lat': gain(9, (DEPTH, MLA_Q_LORA)),
        'w_uq': nrm(10, (DEPTH, MLA_Q_LORA, MLA_HEADS, MLA_QK)) * MLA_Q_LORA ** -0.5,
        'g_kv_lat': gain(11, (DEPTH, MLA_KV_LORA)),
        'w_uk': nrm(12, (DEPTH, MLA_KV_LORA, MLA_HEADS, MLA_NOPE)) * MLA_KV_LORA ** -0.5,
        'w_uv': nrm(13, (DEPTH, MLA_KV_LORA, MLA_HEADS, MLA_V)) * MLA_KV_LORA ** -0.5,
        'g_mla_q': gain(14, (DEPTH, MLA_QK)),
        'g_mla_k': gain(15, (DEPTH, MLA_QK)),
        'g_diff_q': gain(16, (DEPTH, DIFF_HEAD_DIM)),
        'g_diff_k': gain(17, (DEPTH, DIFF_HEAD_DIM)),
        'lambda_q1': 0.1 * nrm(18, (DEPTH, DIFF_HEAD_DIM)),
        'lambda_k1': 0.1 * nrm(19, (DEPTH, DIFF_HEAD_DIM)),
        'lambda_q2': 0.1 * nrm(20, (DEPTH, DIFF_HEAD_DIM)),
        'lambda_k2': 0.1 * nrm(21, (DEPTH, DIFF_HEAD_DIM)),
        'g_subln': gain(22, (DEPTH, DIFF_V)),
        'w_branch_a': nrm(23, (DEPTH, MLA_HEADS * MLA_V, D_MODEL)) * (MLA_HEADS * MLA_V) ** -0.5,
        'w_branch_b': nrm(24, (DEPTH, DIFF_HEADS * DIFF_V, D_MODEL)) * (DIFF_HEADS * DIFF_V) ** -0.5,
        'w_o': nrm(25, (DEPTH, D_MODEL, D_MODEL)) * D_MODEL ** -0.5,
        'g_ffn': gain(26, (DEPTH, D_MODEL)),
        'w_router': nrm(27, (DEPTH, D_MODEL, N_EXPERTS)) * D_MODEL ** -0.5,
        'b_router': 0.01 * nrm(28, (DEPTH, N_EXPERTS)),
        'w_gate_up': nrm(29, (DEPTH, N_EXPERTS, D_MODEL, 2 * D_FF)) * D_MODEL ** -0.5,
        'b_gate_up': 0.01 * nrm(30, (DEPTH, N_EXPERTS, 2 * D_FF)),
        'w_down': nrm(31, (DEPTH, N_EXPERTS, D_FF, D_MODEL)) * D_FF ** -0.5,
        'b_down': 0.01 * nrm(32, (DEPTH, N_EXPERTS, D_MODEL)),
    }


def reference(x_prompt, x_sample, cache_mla_latent, cache_mla_krope, cache_diff_k, cache_diff_v, page_table,
              meta_tokens, g_attn, w_in, g_q_lat, w_uq, g_kv_lat, w_uk, w_uv, g_mla_q, g_mla_k, g_diff_q,
              g_diff_k, lambda_q1, lambda_k1, lambda_q2, lambda_k2, g_subln, w_branch_a, w_branch_b, w_o,
              g_ffn, w_router, b_router, w_gate_up, b_gate_up, w_down, b_down):
    f32 = jnp.float32
    b, seq = x_prompt.shape[:2]
    pos_p = jnp.arange(N_META + seq, dtype=jnp.int32)
    n_pages = page_table.shape[1]
    past_len = n_pages * cache_mla_latent.shape[2]
    dec_seq = x_sample.shape[1]
    kpos_s = jnp.arange(past_len + dec_seq, dtype=jnp.int32)
    pos_s = kpos_s[past_len:]

    meta = jnp.broadcast_to(meta_tokens[None], (b, N_META, meta_tokens.shape[-1])).astype(x_prompt.dtype)
    x_p = jnp.concatenate([meta, x_prompt], axis=1)
    x_s = x_sample
    lat_p, kpe_p, dk_p, dv_p = [], [], [], []
    lat_s, kpe_s, dk_s, dv_s = [], [], [], []

    for l in range(DEPTH):
        lp = {'g_attn': g_attn[l], 'w_in': w_in[l], 'g_q_lat': g_q_lat[l], 'w_uq': w_uq[l],
              'g_kv_lat': g_kv_lat[l], 'w_uk': w_uk[l], 'w_uv': w_uv[l], 'g_mla_q': g_mla_q[l],
              'g_mla_k': g_mla_k[l], 'g_diff_q': g_diff_q[l], 'g_diff_k': g_diff_k[l], 'g_subln': g_subln[l],
              'w_branch_a': w_branch_a[l], 'w_branch_b': w_branch_b[l], 'w_o': w_o[l], 'g_ffn': g_ffn[l],
              'w_router': w_router[l], 'b_router': b_router[l], 'w_gate_up': w_gate_up[l],
              'b_gate_up': b_gate_up[l], 'w_down': w_down[l], 'b_down': b_down[l]}
        lambda_init = 0.8 - 0.6 * math.exp(-0.3 * l)
        lam = (jnp.exp(jnp.sum(lambda_q1[l].astype(f32) * lambda_k1[l].astype(f32)))
               - jnp.exp(jnp.sum(lambda_q2[l].astype(f32) * lambda_k2[l].astype(f32))) + lambda_init)

        q_a, lat, kpe, q_d, k_d, v_d, g_a, g_b = mixer_inputs(x_p, pos_p, lp)
        k_a, v_a = mla_keys_values(lat, kpe, lp)
        o_a = sweep_query_blocks(lambda q, qp: mla_attend(q, qp, k_a, v_a, pos_p), q_a, pos_p)
        o_d = sweep_query_blocks(lambda q, qp: diff_attend(q, qp, k_d, v_d, pos_p, lam), q_d, pos_p)
        x_p = moe_ffn(merge_branches(x_p, o_a, o_d, g_a, g_b, lp, lambda_init), lp)
        lat_p.append(lat)
        kpe_p.append(kpe)
        dk_p.append(k_d)
        dv_p.append(v_d)

        sq_a, slat, skpe, sq_d, sk_d, sv_d, sg_a, sg_b = mixer_inputs(x_s, pos_s, lp)

        def sample_seq(args):
            pt, qa, lat_n, kpe_n, qd, kd_n, vd_n = args
            lat_all = jnp.concatenate([cache_mla_latent[l, pt].reshape(past_len, MLA_KV_LORA), lat_n], 0)[None]
            kpe_all = jnp.concatenate([cache_mla_krope[l, pt].reshape(past_len, MLA_ROPE), kpe_n], 0)[None]
            ka, va = mla_keys_values(lat_all, kpe_all, lp)
            oa = mla_attend(qa[None], pos_s, ka, va, kpos_s)[0]
            kd = jnp.concatenate([cache_diff_k[l, pt].reshape((past_len,) + kd_n.shape[1:]), kd_n], 0)[None]
            vd = jnp.concatenate([cache_diff_v[l, pt].reshape((past_len,) + vd_n.shape[1:]), vd_n], 0)[None]
            od = diff_attend(qd[None], pos_s, kd, vd, kpos_s, lam)[0]
            return oa, od

        so_a, so_d = lax.map(sample_seq, (page_table, sq_a, slat, skpe, sq_d, sk_d, sv_d))
        x_s = moe_ffn(merge_branches(x_s, so_a, so_d, sg_a, sg_b, lp, lambda_init), lp)
        lat_s.append(slat)
        kpe_s.append(skpe)
        dk_s.append(sk_d)
        dv_s.append(sv_d)

    y_prompt = x_p[:, N_META:]
    y_sample = x_s
    return (y_prompt, y_sample, jnp.stack(lat_p), jnp.stack(kpe_p), jnp.stack(dk_p), jnp.stack(dv_p),
            jnp.stack(lat_s), jnp.stack(kpe_s), jnp.stack(dk_s), jnp.stack(dv_s))
```

```python
import functools
import math

import jax
import jax.numpy as jnp
from jax import lax
from jax.experimental import pallas as pl
from jax.experimental.pallas import tpu as pltpu

F32 = jnp.float32
BF16 = jnp.bfloat16

D_MODEL = 1024
N_META = 16
EPS = 1e-6
MLA_HEADS = 8
MLA_NOPE = 64
MLA_ROPE = 32
MLA_QK = MLA_NOPE + MLA_ROPE
MLA_V = 64
MLA_Q_LORA = 256
MLA_KV_LORA = 128
MLA_SCALE = 1.0 / math.sqrt(MLA_QK)
ROPE_BASE = 10000.0
DIFF_HEADS = 8
DIFF_KV_HEADS = 4
DIFF_REP = DIFF_HEADS // DIFF_KV_HEADS
DIFF_HEAD_DIM = 64
DIFF_V = 2 * DIFF_HEAD_DIM
DIFF_SCALE = 1.0 / math.sqrt(DIFF_HEAD_DIM)
N_EXPERTS = 32
TOP_K = 4
D_FF = D_MODEL
SWIGLU_LIMIT = 7.0
SWIGLU_ALPHA = 1.702
LANE = 128
VMEM_LIMIT = 56 * 1024 * 1024

NEG_INF = float("-inf")
_NT = (((1,), (1,)), ((), ()))


def _dot(a, b):
    return jnp.dot(a, b, preferred_element_type=F32)


def _dot_nt(a, b):
    return lax.dot_general(a, b, _NT, preferred_element_type=F32)


def _rms(x, g):
    return x * lax.rsqrt(jnp.mean(x * x, axis=-1, keepdims=True) + EPS) * g


def _const_spec(shape):
    nd = len(shape)
    return pl.BlockSpec(shape, lambda *_: (0,) * nd)


def _inproj_kernel(x_ref, cq_ref, sq_ref, gattn_ref, wa_ref, gql_ref, wq_ref, wqs_ref, gq_ref, gkv_ref,
                   wuk_ref, gk_ref, wuv_ref, wqd_ref, gdq_ref, wkd_ref, gdk_ref, wvd_ref, wga_ref, wgb_ref,
                   b2_ref,
                   qa_ref, lat_ref, kpe_ref, kmla_ref, vmla_ref, qd_ref, kd_ref, kdb_ref, vd_ref, vdb_ref,
                   sga_ref, sgb_ref):
    xn = _rms(x_ref[...], gattn_ref[...]).astype(BF16)
    cq = cq_ref[...]
    sq = sq_ref[...]
    za = _dot(xn, wa_ref[...])
    lat = _rms(za[:, 256:384], gkv_ref[...])
    lat_ref[...] = lat
    kpe_pad = za[:, 384:512] * cq + za[:, 512:640] * sq
    kpe_ref[...] = kpe_pad[:, MLA_NOPE:MLA_QK]

    qn = _rms(za[:, :MLA_Q_LORA], gql_ref[...]).astype(BF16)
    q = _dot(qn, wq_ref[...])
    qs = _dot(qn, wqs_ref[...])
    gq = gq_ref[...]
    for h in range(MLA_HEADS):
        sl = slice(h * LANE, (h + 1) * LANE)
        blk = q[:, sl] * cq + qs[:, sl] * sq
        ss = jnp.sum(blk * blk, axis=-1, keepdims=True) * (1.0 / MLA_QK)
        qa_ref[:, sl] = (blk * lax.rsqrt(ss + EPS) * gq).astype(BF16)

    latb = lat.astype(BF16)
    kn = _dot(latb, wuk_ref[...])
    gk = gk_ref[...]
    for h in range(MLA_HEADS):
        sl = slice(h * LANE, (h + 1) * LANE)
        blk = kn[:, sl] + kpe_pad
        ss = jnp.sum(blk * blk, axis=-1, keepdims=True) * (1.0 / MLA_QK)
        kmla_ref[:, sl] = (blk * lax.rsqrt(ss + EPS) * gk).astype(BF16)
    vmla_ref[...] = _dot(latb, wuv_ref[...]).astype(BF16)

    b2 = b2_ref[...]
    zq = _dot(xn, wqd_ref[...])
    gdq = gdq_ref[...]
    for j in range(DIFF_HEADS):
        sl = slice(j * LANE, (j + 1) * LANE)
        blk = zq[:, sl]
        ss = _dot((blk * blk).astype(BF16), b2) * (1.0 / DIFF_HEAD_DIM)
        qd_ref[:, sl] = (blk * lax.rsqrt(ss + EPS) * gdq).astype(BF16)
    zk = _dot(xn, wkd_ref[...])
    gdk = gdk_ref[...]
    for j in range(DIFF_KV_HEADS):
        sl = slice(j * LANE, (j + 1) * LANE)
        blk = zk[:, sl]
        ss = _dot((blk * blk).astype(BF16), b2) * (1.0 / DIFF_HEAD_DIM)
        kd = blk * lax.rsqrt(ss + EPS) * gdk
        kd_ref[:, sl] = kd
        kdb_ref[:, sl] = kd.astype(BF16)
    zv = _dot(xn, wvd_ref[...])
    vd_ref[...] = zv
    vdb_ref[...] = zv.astype(BF16)
    sga_ref[...] = jax.nn.sigmoid(_dot(xn, wga_ref[...])).astype(BF16)
    sgb_ref[...] = jax.nn.sigmoid(_dot(xn, wgb_ref[...])).astype(BF16)


def _inproj(x, cq, sq, wts, tm):
    rows = x.shape[0]
    n_tab = cq.shape[0] // tm
    row = lambda i: (i, 0)
    tab = lambda i: (i % n_tab, 0)
    in_specs = [pl.BlockSpec((tm, D_MODEL), row), pl.BlockSpec((tm, LANE), tab), pl.BlockSpec((tm, LANE), tab)]
    in_specs += [_const_spec(w.shape) for w in wts]
    widths = [(8 * LANE, BF16), (MLA_KV_LORA, F32), (MLA_ROPE, F32), (8 * LANE, BF16), (MLA_HEADS * MLA_V, BF16),
              (8 * LANE, BF16), (4 * LANE, F32), (4 * LANE, BF16), (4 * LANE, F32), (4 * LANE, BF16),
              (D_MODEL, BF16), (D_MODEL, BF16)]
    return pl.pallas_call(
        _inproj_kernel,
        grid=(rows // tm,),
        in_specs=in_specs,
        out_specs=[pl.BlockSpec((tm, w), row) for w, _ in widths],
        out_shape=[jax.ShapeDtypeStruct((rows, w), dt) for w, dt in widths],
        compiler_params=pltpu.CompilerParams(dimension_semantics=("parallel",), vmem_limit_bytes=VMEM_LIMIT),
        name="inproj",
    )(x, cq, sq, *wts)


def _softmax_update(s, v, m_ref, l_ref, acc_ref, idx):
    m_old = m_ref[idx]
    m_new = jnp.maximum(m_old, jnp.max(s, axis=-1, keepdims=True))
    alpha = jnp.exp(m_old - m_new)
    p = jnp.exp(s - m_new)
    l_ref[idx] = alpha * l_ref[idx] + jnp.sum(p, axis=-1, keepdims=True)
    acc_ref[idx] = alpha * acc_ref[idx] + _dot(p.astype(BF16), v)
    m_ref[idx] = m_new


def _init_state(m_ref, l_ref, acc_ref):
    m_ref[...] = jnp.full(m_ref.shape, NEG_INF, F32)
    l_ref[...] = jnp.zeros(l_ref.shape, F32)
    acc_ref[...] = jnp.zeros(acc_ref.shape, F32)


def _mla_flash_kernel(q_ref, k_ref, v_ref, km_ref, vm_ref, o_ref, m_ref, l_ref, acc_ref, *, tq):
    qi = pl.program_id(2)
    _init_state(m_ref, l_ref, acc_ref)
    qs = [q_ref[:, h * LANE:(h + 1) * LANE] for h in range(2)]

    col = lax.broadcasted_iota(jnp.int32, (tq, LANE), 1)
    vm = vm_ref[...]
    for h in range(2):
        s = _dot_nt(qs[h], km_ref[:, h * LANE:(h + 1) * LANE])
        _softmax_update(jnp.where(col < N_META, s, NEG_INF), vm, m_ref, l_ref, acc_ref, h)

    def full_tile(j, carry):
        start = pl.multiple_of(j * tq, tq)
        v = v_ref[pl.ds(start, tq), :]
        for h in range(2):
            s = _dot_nt(qs[h], k_ref[pl.ds(start, tq), h * LANE:(h + 1) * LANE])
            _softmax_update(s, v, m_ref, l_ref, acc_ref, h)
        return carry

    lax.fori_loop(0, qi, full_tile, 0)

    start = pl.multiple_of(qi * tq, tq)
    v = v_ref[pl.ds(start, tq), :]
    causal = lax.broadcasted_iota(jnp.int32, (tq, tq), 1) <= lax.broadcasted_iota(jnp.int32, (tq, tq), 0)
    for h in range(2):
        s = _dot_nt(qs[h], k_ref[pl.ds(start, tq), h * LANE:(h + 1) * LANE])
        _softmax_update(jnp.where(causal, s, NEG_INF), v, m_ref, l_ref, acc_ref, h)

    o0 = acc_ref[0] / l_ref[0]
    o1 = acc_ref[1] / l_ref[1]
    o_ref[...] = jnp.where(col < MLA_V, o0, o1).astype(BF16)


def _mla_flash(qa, kmla, vmla, km, vm, batch, seq, tq):
    nq = seq // tq
    return pl.pallas_call(
        functools.partial(_mla_flash_kernel, tq=tq),
        grid=(batch, MLA_HEADS // 2, nq),
        in_specs=[
            pl.BlockSpec((tq, 2 * LANE), lambda b, hp, i: (b * nq + i, hp)),
            pl.BlockSpec((seq, 2 * LANE), lambda b, hp, i: (b, hp)),
            pl.BlockSpec((seq, LANE), lambda b, hp, i: (b, hp)),
            pl.BlockSpec((LANE, 2 * LANE), lambda b, hp, i: (0, hp)),
            pl.BlockSpec((LANE, LANE), lambda b, hp, i: (0, hp)),
        ],
        out_specs=pl.BlockSpec((tq, LANE), lambda b, hp, i: (b * nq + i, hp)),
        out_shape=jax.ShapeDtypeStruct((batch * seq, MLA_HEADS * MLA_V), BF16),
        scratch_shapes=[pltpu.VMEM((2, tq, 1), F32), pltpu.VMEM((2, tq, 1), F32), pltpu.VMEM((2, tq, LANE), F32)],
        compiler_params=pltpu.CompilerParams(dimension_semantics=("parallel", "parallel", "parallel"),
                                             vmem_limit_bytes=VMEM_LIMIT),
        name="mla_flash",
    )(qa, kmla, vmla, km, vm)


def _lambda_value(lq1_ref, lk1_ref, lq2_ref, lk2_ref, lambda_init):
    a = jnp.sum(lq1_ref[...] * lk1_ref[...], axis=-1, keepdims=True)
    b = jnp.sum(lq2_ref[...] * lk2_ref[...], axis=-1, keepdims=True)
    return jnp.exp(a) - jnp.exp(b) + lambda_init


def _diff_flash_kernel(slopes_ref, q_ref, k_ref, v_ref, km_ref, vm_ref, gsub_ref, lq1_ref, lk1_ref, lq2_ref,
                       lk2_ref, o_ref, m_ref, l_ref, acc_ref, *, tq, lambda_init):
    g = pl.program_id(1)
    qi = pl.program_id(2)
    _init_state(m_ref, l_ref, acc_ref)
    lane = lax.broadcasted_iota(jnp.int32, (tq, LANE), 1)
    qs = []
    slopes = []
    for r in range(DIFF_REP):
        qh = q_ref[:, r * LANE:(r + 1) * LANE].astype(F32)
        qs.append(jnp.where(lane < DIFF_HEAD_DIM, qh, 0.0).astype(BF16))
        qs.append(jnp.where(lane >= DIFF_HEAD_DIM, qh, 0.0).astype(BF16))
        slopes += [slopes_ref[g * DIFF_REP + r]] * 2

    q0 = N_META + qi * tq
    kcol = lax.broadcasted_iota(jnp.int32, (1, LANE), 1)
    km = km_ref[...]
    vm = vm_ref[...]
    for i in range(4):
        s = _dot_nt(qs[i], km) + slopes[i] * (kcol - q0).astype(F32)
        _softmax_update(jnp.where(lane < N_META, s, NEG_INF), vm, m_ref, l_ref, acc_ref, i)

    tcol = lax.broadcasted_iota(jnp.int32, (1, tq), 1)

    def full_tile(j, carry):
        start = pl.multiple_of(j * tq, tq)
        k = k_ref[pl.ds(start, tq), :]
        v = v_ref[pl.ds(start, tq), :]
        rel = (tcol + (j - qi) * tq).astype(F32)
        for i in range(4):
            _softmax_update(_dot_nt(qs[i], k) + slopes[i] * rel, v, m_ref, l_ref, acc_ref, i)
        return carry

    lax.fori_loop(0, qi, full_tile, 0)

    start = pl.multiple_of(qi * tq, tq)
    k = k_ref[pl.ds(start, tq), :]
    v = v_ref[pl.ds(start, tq), :]
    causal = lax.broadcasted_iota(jnp.int32, (tq, tq), 1) <= lax.broadcasted_iota(jnp.int32, (tq, tq), 0)
    rel = tcol.astype(F32)
    for i in range(4):
        s = _dot_nt(qs[i], k) + slopes[i] * rel
        _softmax_update(jnp.where(causal, s, NEG_INF), v, m_ref, l_ref, acc_ref, i)

    lam = _lambda_value(lq1_ref, lk1_ref, lq2_ref, lk2_ref, lambda_init)
    gsub = gsub_ref[...] * (1.0 - lambda_init)
    for r in range(DIFF_REP):
        o = acc_ref[2 * r] / l_ref[2 * r] - lam * (acc_ref[2 * r + 1] / l_ref[2 * r + 1])
        o_ref[:, r * LANE:(r + 1) * LANE] = _rms(o, gsub).astype(BF16)


def _diff_flash(slopes, qd, kdb, vdb, km, vm, gsub, lams, batch, seq, tq, lambda_init):
    nq = seq // tq
    vec = pl.BlockSpec((1, DIFF_HEAD_DIM), lambda b, g, i, s: (0, 0))
    return pl.pallas_call(
        functools.partial(_diff_flash_kernel, tq=tq, lambda_init=lambda_init),
        grid_spec=pltpu.PrefetchScalarGridSpec(
            num_scalar_prefetch=1,
            grid=(batch, DIFF_KV_HEADS, nq),
            in_specs=[
                pl.BlockSpec((tq, 2 * LANE), lambda b, g, i, s: (b * nq + i, g)),
                pl.BlockSpec((seq, LANE), lambda b, g, i, s: (b, g)),
                pl.BlockSpec((seq, LANE), lambda b, g, i, s: (b, g)),
                pl.BlockSpec((LANE, LANE), lambda b, g, i, s: (0, g)),
                pl.BlockSpec((LANE, LANE), lambda b, g, i, s: (0, g)),
                pl.BlockSpec((1, LANE), lambda b, g, i, s: (0, 0)),
                vec, vec, vec, vec,
            ],
            out_specs=pl.BlockSpec((tq, 2 * LANE), lambda b, g, i, s: (b * nq + i, g)),
            scratch_shapes=[pltpu.VMEM((4, tq, 1), F32), pltpu.VMEM((4, tq, 1), F32),
                            pltpu.VMEM((4, tq, LANE), F32)],
        ),
        out_shape=jax.ShapeDtypeStruct((batch * seq, DIFF_HEADS * DIFF_V), BF16),
        compiler_params=pltpu.CompilerParams(dimension_semantics=("parallel", "parallel", "parallel"),
                                             vmem_limit_bytes=VMEM_LIMIT),
        name="diff_flash",
    )(slopes, qd, kdb, vdb, km, vm, gsub, *lams)


def _sample_qprep_kernel(qa_ref, gk_ref, wabs_ref, sel_ref, qabs_ref, qrope_ref):
    qg = (qa_ref[...].astype(F32) * gk_ref[...]).astype(BF16)
    qabs_ref[0] = _dot(qg, wabs_ref[0])
    qrope_ref[0] = _dot(qg, sel_ref[...])


def _sample_qprep(qa_s, gk_pad, wabs, sel):
    rows = qa_s.shape[0]
    return pl.pallas_call(
        _sample_qprep_kernel,
        grid=(MLA_HEADS,),
        in_specs=[pl.BlockSpec((rows, LANE), lambda h: (0, h)), _const_spec((1, LANE)),
                  pl.BlockSpec((1, LANE, MLA_KV_LORA), lambda h: (h, 0, 0)), _const_spec((LANE, MLA_ROPE))],
        out_specs=[pl.BlockSpec((1, rows, MLA_KV_LORA), lambda h: (h, 0, 0)),
                   pl.BlockSpec((1, rows, MLA_ROPE), lambda h: (h, 0, 0))],
        out_shape=[jax.ShapeDtypeStruct((MLA_HEADS, rows, MLA_KV_LORA), F32),
                   jax.ShapeDtypeStruct((MLA_HEADS, rows, MLA_ROPE), F32)],
        compiler_params=pltpu.CompilerParams(dimension_semantics=("parallel",)),
        name="sample_qprep",
    )(qa_s, gk_pad, wabs, sel)


def _sample_attn_kernel(pt_ref, qabs_ref, qrope_ref, qd_ref, latn_ref, kpen_ref, kdn_ref, vdn_ref,
                        wukt_ref, bsel_ref, ones_ref, wuvw_ref, slope_ref, qpos_ref, gsub_ref,
                        lq1_ref, lk1_ref, lq2_ref, lk2_ref, hmask_ref, maska_ref, maskd_ref,
                        clat_ref, ckpe_ref, ckd_ref, cvd_ref,
                        oa_ref, od_ref,
                        lat_buf, kpe_buf, kd_buf, vd_buf, sem, lhs_a, qbd_f, qbd,
                        ma_ref, la_ref, acca_ref, md_ref, ld_ref, accd_ref,
                        *, n_pages, ppc, page, dec_seq, lambda_init):
    n_chunks = n_pages // ppc
    ck = ppc * page
    nq_rows = MLA_HEADS * dec_seq
    past_len = n_pages * page
    caches = (clat_ref, ckpe_ref, ckd_ref, cvd_ref)
    bufs = (lat_buf, kpe_buf, kd_buf, vd_buf)

    def page_copy(a, c, p, slot):
        pg = pt_ref[0, 0, c * ppc + p]
        return pltpu.make_async_copy(caches[a].at[pg], bufs[a].at[slot, pl.ds(p * page, page)], sem.at[slot, a])

    def issue(c, slot):
        for p in range(ppc):
            for a in range(4):
                page_copy(a, c, p, slot).start()

    def wait(c, slot):
        for p in range(ppc):
            for a in range(4):
                page_copy(a, c, p, slot).wait()

    issue(0, 0)

    lhs_a[0:MLA_HEADS * MLA_NOPE, :] = wukt_ref[...]
    lhs_a[MLA_HEADS * MLA_NOPE:, :] = qabs_ref[...].reshape(nq_rows, MLA_KV_LORA).astype(BF16)
    qrope = qrope_ref[...].reshape(nq_rows, MLA_ROPE).astype(BF16)
    qbd_f[...] = jnp.zeros(qbd_f.shape, F32)
    lane = lax.broadcasted_iota(jnp.int32, (dec_seq, LANE), 1)
    for c in range(2):
        for g in range(DIFF_KV_HEADS):
            for r in range(DIFF_REP):
                h = g * DIFF_REP + r
                src = qd_ref[:, h * LANE:(h + 1) * LANE]
                keep = (lane >= DIFF_HEAD_DIM) if c else (lane < DIFF_HEAD_DIM)
                row0 = ((c * DIFF_KV_HEADS + g) * DIFF_REP + r) * dec_seq
                qbd_f[row0:row0 + dec_seq, g * LANE:(g + 1) * LANE] = jnp.where(keep, src, 0.0)
    qbd[...] = qbd_f[...].astype(BF16)
    _init_state(ma_ref, la_ref, acca_ref)
    _init_state(md_ref, ld_ref, accd_ref)
    slope = slope_ref[...]
    qpos = qpos_ref[...]

    def attend(lat, kpe, kd, vd, kpos0, nkeys, mask_a, mask_d):
        latb = lat.astype(BF16)
        r_all = _dot_nt(lhs_a[...], latb)
        kn = r_all[0:MLA_HEADS * MLA_NOPE]
        ssq = _dot(bsel_ref[...], (kn * kn).astype(BF16)) + _dot_nt(ones_ref[...], (kpe * kpe).astype(BF16))
        rinv = lax.rsqrt(ssq * (1.0 / MLA_QK) + EPS)
        s = (r_all[MLA_HEADS * MLA_NOPE:] + _dot_nt(qrope, kpe.astype(BF16))) * rinv
        if mask_a is not None:
            s = jnp.where(mask_a > 0.0, s, NEG_INF)
        _softmax_update(s, latb, ma_ref, la_ref, acca_ref, 0)

        kpos = (kpos0 + lax.broadcasted_iota(jnp.int32, (1, nkeys), 1)).astype(F32)
        sd = _dot_nt(qbd[...], kd.astype(BF16)) - slope * (qpos - kpos)
        if mask_d is not None:
            sd = jnp.where(mask_d > 0.0, sd, NEG_INF)
        _softmax_update(sd, vd.astype(BF16), md_ref, ld_ref, accd_ref, 0)

    def chunk(c, carry):
        slot = c % 2

        @pl.when(c + 1 < n_chunks)
        def _():
            issue(c + 1, 1 - slot)

        wait(c, slot)
        attend(lat_buf[slot], kpe_buf[slot], kd_buf[slot], vd_buf[slot], c * ck, ck, None, None)
        return carry

    lax.fori_loop(0, n_chunks, chunk, 0)

    def pad(x):
        return jnp.concatenate([x, jnp.zeros((LANE - dec_seq, x.shape[1]), x.dtype)], axis=0)

    attend(pad(latn_ref[...]), pad(kpen_ref[...]), pad(kdn_ref[...]), pad(vdn_ref[...]), past_len, LANE,
           maska_ref[...], maskd_ref[...])

    o_lat = (acca_ref[0] / la_ref[0]).astype(BF16)
    res = _dot(o_lat, wuvw_ref[...]) * hmask_ref[...]
    out = res[0:dec_seq]
    for h in range(1, MLA_HEADS):
        out = out + res[h * dec_seq:(h + 1) * dec_seq]
    oa_ref[...] = out

    lam = _lambda_value(lq1_ref, lk1_ref, lq2_ref, lk2_ref, lambda_init)
    gsub = gsub_ref[...] * (1.0 - lambda_init)
    half = DIFF_HEADS * dec_seq
    for g in range(DIFF_KV_HEADS):
        for r in range(DIFF_REP):
            h = g * DIFF_REP + r
            r0 = (g * DIFF_REP + r) * dec_seq
            cs = slice(g * LANE, (g + 1) * LANE)
            o0 = accd_ref[0, r0:r0 + dec_seq, cs] / ld_ref[0, r0:r0 + dec_seq]
            o1 = accd_ref[0, half + r0:half + r0 + dec_seq, cs] / ld_ref[0, half + r0:half + r0 + dec_seq]
            od_ref[:, h * LANE:(h + 1) * LANE] = _rms(o0 - lam * o1, gsub)


def _sample_attn(page_table, qabs, qrope, qd_s, lat_s, kpe_s, kd_s, vd_s, consts, caches, dec_seq, ppc,
                 lambda_init):
    n_seq, n_pages = page_table.shape
    page = caches[0].shape[1]
    ck = ppc * page
    pt3 = page_table.reshape(n_seq, 1, n_pages)
    per_seq = lambda w: pl.BlockSpec((dec_seq, w), lambda s: (s, 0))
    any_spec = pl.BlockSpec(memory_space=pl.ANY)
    in_specs = [
        pl.BlockSpec((1, 1, n_pages), lambda s: (s, 0, 0), memory_space=pltpu.SMEM),
        pl.BlockSpec((MLA_HEADS, dec_seq, MLA_KV_LORA), lambda s: (0, s, 0)),
        pl.BlockSpec((MLA_HEADS, dec_seq, MLA_ROPE), lambda s: (0, s, 0)),
        per_seq(DIFF_HEADS * DIFF_V), per_seq(MLA_KV_LORA), per_seq(MLA_ROPE),
        per_seq(DIFF_KV_HEADS * DIFF_V), per_seq(DIFF_KV_HEADS * DIFF_V),
    ] + [_const_spec(c.shape) for c in consts] + [any_spec] * 4
    n_drow = 2 * DIFF_HEADS * dec_seq
    return pl.pallas_call(
        functools.partial(_sample_attn_kernel, n_pages=n_pages, ppc=ppc, page=page, dec_seq=dec_seq,
                          lambda_init=lambda_init),
        grid=(n_seq,),
        in_specs=in_specs,
        out_specs=[per_seq(MLA_HEADS * MLA_V), per_seq(DIFF_HEADS * DIFF_V)],
        out_shape=[jax.ShapeDtypeStruct((n_seq * dec_seq, MLA_HEADS * MLA_V), F32),
                   jax.ShapeDtypeStruct((n_seq * dec_seq, DIFF_HEADS * DIFF_V), F32)],
        scratch_shapes=[
            pltpu.VMEM((2, ck, MLA_KV_LORA), F32), pltpu.VMEM((2, ck, MLA_ROPE), F32),
            pltpu.VMEM((2, ck, DIFF_KV_HEADS * DIFF_V), F32), pltpu.VMEM((2, ck, DIFF_KV_HEADS * DIFF_V), F32),
            pltpu.SemaphoreType.DMA((2, 4)),
            pltpu.VMEM((MLA_HEADS * MLA_NOPE + MLA_HEADS * dec_seq, MLA_KV_LORA), BF16),
            pltpu.VMEM((n_drow, DIFF_KV_HEADS * DIFF_V), F32),
            pltpu.VMEM((n_drow, DIFF_KV_HEADS * DIFF_V), BF16),
            pltpu.VMEM((1, MLA_HEADS * dec_seq, 1), F32), pltpu.VMEM((1, MLA_HEADS * dec_seq, 1), F32),
            pltpu.VMEM((1, MLA_HEADS * dec_seq, MLA_KV_LORA), F32),
            pltpu.VMEM((1, n_drow, 1), F32), pltpu.VMEM((1, n_drow, 1), F32),
            pltpu.VMEM((1, n_drow, DIFF_KV_HEADS * DIFF_V), F32),
        ],
        compiler_params=pltpu.CompilerParams(dimension_semantics=("arbitrary",), vmem_limit_bytes=VMEM_LIMIT),
        name="sample_attn",
    )(pt3, qabs, qrope, qd_s, lat_s, kpe_s, kd_s, vd_s, *consts, *caches)


def _merge_kernel(x_ref, oa_ref, od_ref, sga_ref, sgb_ref, wba_ref, wbb_ref, wo_ref, gffn_ref, wrh_ref, wrl_ref,
                  br_ref, x1_ref, h_ref, gates_ref):
    ya = _dot(oa_ref[...].astype(BF16), wba_ref[...])
    yb = _dot(od_ref[...].astype(BF16), wbb_ref[...])
    m = sga_ref[...].astype(F32) * ya + sgb_ref[...].astype(F32) * yb
    x1 = x_ref[...] + _dot(m.astype(BF16), wo_ref[...])
    x1_ref[...] = x1
    h = _rms(x1, gffn_ref[...])
    hb = h.astype(BF16)
    h_ref[...] = hb
    hl = (h - hb.astype(F32)).astype(BF16)
    logits = _dot(hb, wrh_ref[...]) + _dot(hl, wrh_ref[...]) + _dot(hb, wrl_ref[...]) + br_ref[...]
    iota = lax.broadcasted_iota(jnp.int32, logits.shape, 1).astype(F32)
    vals, hots = [], []
    for _ in range(TOP_K):
        mx = jnp.max(logits, axis=-1, keepdims=True)
        first = jnp.min(jnp.where(logits == mx, iota, float(N_EXPERTS)), axis=-1, keepdims=True)
        hot = iota == first
        vals.append(mx)
        hots.append(hot)
        logits = jnp.where(hot, NEG_INF, logits)
    es = [jnp.exp(v - vals[0]) for v in vals]
    den = es[0] + es[1] + es[2] + es[3]
    gates = jnp.zeros(logits.shape, F32)
    for e, hot in zip(es, hots):
        gates = jnp.where(hot, e / den, gates)
    gates_ref[...] = gates


def _merge(x, oa, od, sga, sgb, wts, tm):
    rows = x.shape[0]
    row = lambda i: (i, 0)
    in_specs = [pl.BlockSpec((tm, D_MODEL), row), pl.BlockSpec((tm, MLA_HEADS * MLA_V), row),
                pl.BlockSpec((tm, DIFF_HEADS * DIFF_V), row), pl.BlockSpec((tm, D_MODEL), row),
                pl.BlockSpec((tm, D_MODEL), row)] + [_const_spec(w.shape) for w in wts]
    return pl.pallas_call(
        _merge_kernel,
        grid=(rows // tm,),
        in_specs=in_specs,
        out_specs=[pl.BlockSpec((tm, D_MODEL), row), pl.BlockSpec((tm, D_MODEL), row),
                   pl.BlockSpec((tm, N_EXPERTS), row)],
        out_shape=[jax.ShapeDtypeStruct((rows, D_MODEL), F32), jax.ShapeDtypeStruct((rows, D_MODEL), BF16),
                   jax.ShapeDtypeStruct((rows, N_EXPERTS), F32)],
        compiler_params=pltpu.CompilerParams(dimension_semantics=("parallel",), vmem_limit_bytes=VMEM_LIMIT),
        name="merge_router",
    )(x, oa, od, sga, sgb, *wts)


def _moe_kernel(h_ref, g_ref, x1_ref, wg_ref, wu_ref, bg_ref, bu_ref, wd_ref, bd_ref, o_ref):
    e = pl.program_id(1)

    @pl.when(e == 0)
    def _():
        o_ref[...] = x1_ref[...]

    h = h_ref[...]
    glu = jnp.minimum(_dot(h, wg_ref[0]) + bg_ref[0], SWIGLU_LIMIT)
    lin = jnp.clip(_dot(h, wu_ref[0]) + bu_ref[0], -SWIGLU_LIMIT, SWIGLU_LIMIT)
    act = glu * jax.nn.sigmoid(SWIGLU_ALPHA * glu) * (lin + 1.0)
    y = _dot(act.astype(BF16), wd_ref[0]) + bd_ref[0]
    gates = g_ref[...]
    iota = lax.broadcasted_iota(jnp.int32, gates.shape, 1)
    ge = jnp.sum(jnp.where(iota == e, gates, 0.0), axis=-1, keepdims=True)
    o_ref[...] += ge * y


def _moe(h, gates, x1, wg, wu, bg, bu, wd, bd, tm):
    rows = h.shape[0]
    tok = lambda i, e: (i, 0)
    exp3 = lambda i, e: (e, 0, 0)
    return pl.pallas_call(
        _moe_kernel,
        grid=(rows // tm, N_EXPERTS),
        in_specs=[pl.BlockSpec((tm, D_MODEL), tok), pl.BlockSpec((tm, N_EXPERTS), tok),
                  pl.BlockSpec((tm, D_MODEL), tok),
                  pl.BlockSpec((1, D_MODEL, D_FF), exp3), pl.BlockSpec((1, D_MODEL, D_FF), exp3),
                  pl.BlockSpec((1, 1, D_FF), exp3), pl.BlockSpec((1, 1, D_FF), exp3),
                  pl.BlockSpec((1, D_FF, D_MODEL), exp3), pl.BlockSpec((1, 1, D_MODEL), exp3)],
        out_specs=pl.BlockSpec((tm, D_MODEL), tok),
        out_shape=jax.ShapeDtypeStruct((rows, D_MODEL), F32),
        compiler_params=pltpu.CompilerParams(dimension_semantics=("parallel", "arbitrary"),
                                             vmem_limit_bytes=VMEM_LIMIT),
        name="moe",
    )(h, gates, x1, wg, wu, bg, bu, wd, bd)


def _rope_tables(pos):
    inv = 1.0 / (ROPE_BASE ** (jnp.arange(0, MLA_ROPE, 2, dtype=F32) / MLA_ROPE))
    ang = pos.astype(F32)[:, None] * inv[None, :]
    cos, sin = jnp.cos(ang), jnp.sin(ang)
    n = pos.shape[0]
    cq = jnp.concatenate([jnp.ones((n, MLA_NOPE), F32), cos, cos, jnp.zeros((n, LANE - MLA_QK), F32)], axis=1)
    sq = jnp.concatenate([jnp.zeros((n, MLA_NOPE), F32), -sin, sin, jnp.zeros((n, LANE - MLA_QK), F32)], axis=1)
    return cq, sq


def _pad_lanes(w, width=LANE):
    return jnp.pad(w, [(0, 0)] * (w.ndim - 1) + [(0, width - w.shape[-1])])


def _pick_tile(n, cap):
    t = cap
    while n % t:
        t //= 2
    return t


def kernel(x_prompt, x_sample, cache_mla_latent, cache_mla_krope, cache_diff_k, cache_diff_v, page_table, meta_tokens, g_attn, w_in, g_q_lat, w_uq, g_kv_lat, w_uk, w_uv, g_mla_q, g_mla_k, g_diff_q, g_diff_k, lambda_q1, lambda_k1, lambda_q2, lambda_k2, g_subln, w_branch_a, w_branch_b, w_o, g_ffn, w_router, b_router, w_gate_up, b_gate_up, w_down, b_down):
    depth = w_in.shape[0]
    assert depth == 1, "single-layer trunk"
    l = 0
    batch, seq, _ = x_prompt.shape
    n_seq, dec_seq, _ = x_sample.shape
    n_pages = page_table.shape[1]
    page = cache_mla_latent.shape[2]
    past_len = n_pages * page
    lambda_init = 0.8 - 0.6 * math.exp(-0.3 * l)

    w = w_in[l]
    o = 0
    parts = []
    for width in (MLA_Q_LORA, MLA_KV_LORA, MLA_ROPE, DIFF_HEADS * 2 * DIFF_HEAD_DIM,
                  DIFF_KV_HEADS * 2 * DIFF_HEAD_DIM, DIFF_KV_HEADS * DIFF_V, D_MODEL, D_MODEL):
        parts.append(w[:, o:o + width])
        o += width
    w_ql, w_kvl, w_kpe, w_qd, w_kd, w_vd, w_ga, w_gb = parts
    half = MLA_ROPE // 2
    w_kpe_sw = jnp.concatenate([w_kpe[:, half:], w_kpe[:, :half]], axis=1)
    place = lambda m: jnp.pad(m, ((0, 0), (MLA_NOPE, LANE - MLA_QK)))
    wa = jnp.concatenate([w_ql, w_kvl, place(w_kpe), place(w_kpe_sw)], axis=1).astype(BF16)
    uq = w_uq[l]
    uq_sw = jnp.concatenate([jnp.zeros_like(uq[..., :MLA_NOPE]), uq[..., MLA_NOPE + half:],
                             uq[..., MLA_NOPE:MLA_NOPE + half]], axis=-1)
    wq = _pad_lanes(uq).reshape(MLA_Q_LORA, MLA_HEADS * LANE).astype(BF16)
    wqs = _pad_lanes(uq_sw).reshape(MLA_Q_LORA, MLA_HEADS * LANE).astype(BF16)
    wuk = _pad_lanes(w_uk[l]).reshape(MLA_KV_LORA, MLA_HEADS * LANE).astype(BF16)
    wuv = w_uv[l].reshape(MLA_KV_LORA, MLA_HEADS * MLA_V).astype(BF16)
    gq = _pad_lanes(g_mla_q[l] * MLA_SCALE)[None]
    gk = _pad_lanes(g_mla_k[l])[None]
    gdq = (jnp.concatenate([g_diff_q[l]] * 2) * DIFF_SCALE)[None]
    gdk = jnp.concatenate([g_diff_k[l]] * 2)[None]
    grp = jnp.arange(LANE) // DIFF_HEAD_DIM
    b2 = (grp[:, None] == grp[None, :]).astype(BF16)
    in_wts = (g_attn[l][None], wa, g_q_lat[l][None], wq, wqs, gq, g_kv_lat[l][None], wuk, gk, wuv,
              w_qd.astype(BF16), gdq, w_kd.astype(BF16), gdk, w_vd.astype(BF16), w_ga.astype(BF16),
              w_gb.astype(BF16), b2)

    tm_p = _pick_tile(seq, 256)
    cq_p, sq_p = _rope_tables(N_META + jnp.arange(seq, dtype=jnp.int32))
    outs_p = _inproj(x_prompt.reshape(batch * seq, D_MODEL), cq_p, sq_p, in_wts, tm_p)
    (qa_p, lat_p, kpe_p, kmla_p, vmla_p, qd_p, kd_p, kdb_p, vd_p, vdb_p, sga_p, sgb_p) = outs_p

    n_s = n_seq * dec_seq
    tm_s = _pick_tile(n_s, 256)
    n_small = -(-(n_s + N_META) // tm_s) * tm_s
    x_small = jnp.concatenate([x_sample.reshape(n_s, D_MODEL), meta_tokens.astype(F32),
                               jnp.zeros((n_small - n_s - N_META, D_MODEL), F32)], axis=0)
    pos_small = jnp.concatenate([jnp.tile(past_len + jnp.arange(dec_seq, dtype=jnp.int32), n_seq),
                                 jnp.arange(N_META, dtype=jnp.int32),
                                 jnp.zeros((n_small - n_s - N_META,), jnp.int32)])
    cq_s, sq_s = _rope_tables(pos_small)
    outs_s = _inproj(x_small, cq_s, sq_s, in_wts, tm_s)
    (qa_s, lat_s, kpe_s, kmla_s, vmla_s, qd_s, kd_s, kdb_s, vd_s, vdb_s, sga_s, sgb_s) = outs_s
    meta = slice(n_s, n_s + N_META)

    tq = _pick_tile(seq, 512)
    pad_keys = lambda a: jnp.pad(a[meta], ((0, LANE - N_META), (0, 0)))
    o_a_p = _mla_flash(qa_p, kmla_p, vmla_p, pad_keys(kmla_s), pad_keys(vmla_s), batch, seq, tq)
    slopes = jnp.exp2(-8.0 * jnp.arange(1, DIFF_HEADS + 1, dtype=F32) / DIFF_HEADS)
    lams = (lambda_q1[l][None], lambda_k1[l][None], lambda_q2[l][None], lambda_k2[l][None])
    gsub = g_subln[l][None]
    o_d_p = _diff_flash(slopes, qd_p, kdb_p, vdb_p, pad_keys(kdb_s), pad_keys(vdb_s), gsub, lams, batch, seq, tq,
                        lambda_init)

    ukt = jnp.transpose(w_uk[l], (1, 2, 0))
    wabs = jnp.pad(ukt, ((0, 0), (0, LANE - MLA_NOPE), (0, 0))).astype(BF16)
    sel = (jnp.arange(LANE)[:, None] == (MLA_NOPE + jnp.arange(MLA_ROPE))[None, :]).astype(BF16)
    qabs, qrope = _sample_qprep(qa_s[:n_s], gk, wabs, sel)
    n_arow = MLA_HEADS * dec_seq
    bsel = (jnp.arange(n_arow)[:, None] // dec_seq == jnp.arange(MLA_HEADS * MLA_NOPE)[None, :] // MLA_NOPE)
    drow = jnp.arange(2 * DIFF_HEADS * dec_seq)
    slope_rows = slopes[(drow // dec_seq) % DIFF_HEADS][:, None]
    qpos_rows = (past_len + drow % dec_seq).astype(F32)[:, None]
    hmask = (jnp.arange(n_arow)[:, None] // dec_seq == jnp.arange(MLA_HEADS * MLA_V)[None, :] // MLA_V).astype(F32)
    new_key = jnp.arange(LANE)[None, :]
    mask_a = (new_key <= (jnp.arange(n_arow) % dec_seq)[:, None]).astype(F32)
    mask_d = (new_key <= (drow % dec_seq)[:, None]).astype(F32)
    consts = (ukt.reshape(MLA_HEADS * MLA_NOPE, MLA_KV_LORA).astype(BF16), bsel.astype(BF16),
              jnp.ones((n_arow, MLA_ROPE), BF16), wuv, slope_rows, qpos_rows, gsub) + lams + (
                  hmask, mask_a, mask_d)
    n_pool = cache_mla_latent.shape[1]
    caches = (cache_mla_latent[l], cache_mla_krope[l],
              cache_diff_k[l].reshape(n_pool, page, DIFF_KV_HEADS * DIFF_V),
              cache_diff_v[l].reshape(n_pool, page, DIFF_KV_HEADS * DIFF_V))
    ppc = _pick_tile(n_pages, 4)
    o_a_s, o_d_s = _sample_attn(page_table, qabs, qrope, qd_s[:n_s].astype(F32), lat_s[:n_s], kpe_s[:n_s], kd_s[:n_s],
                                vd_s[:n_s], consts, caches, dec_seq, ppc, lambda_init)

    wr = w_router[l]
    wr_hi = wr.astype(BF16)
    wr_lo = (wr - wr_hi.astype(F32)).astype(BF16)
    merge_wts = (w_branch_a[l].astype(BF16), w_branch_b[l].astype(BF16), w_o[l].astype(BF16), g_ffn[l][None],
                 wr_hi, wr_lo, b_router[l][None].astype(F32))
    wgu = w_gate_up[l]
    wg = wgu[:, :, 0::2].astype(BF16)
    wu = wgu[:, :, 1::2].astype(BF16)
    bg = b_gate_up[l][:, None, 0::2]
    bu = b_gate_up[l][:, None, 1::2]
    wd = w_down[l].astype(BF16)
    bd = b_down[l][:, None, :]

    def channel(x, oa, od, sga, sgb, cap):
        x1, h, gates = _merge(x, oa, od, sga, sgb, merge_wts, _pick_tile(x.shape[0], 256))
        return _moe(h, gates, x1, wg, wu, bg, bu, wd, bd, _pick_tile(x.shape[0], cap))

    y_p = channel(x_prompt.reshape(batch * seq, D_MODEL), o_a_p, o_d_p, sga_p, sgb_p, 512)
    y_s = channel(x_sample.reshape(n_s, D_MODEL), o_a_s, o_d_s, sga_s[:n_s], sgb_s[:n_s], 512)

    def prompt_rows(real, small, tail):
        real = real.reshape((batch, seq) + tail)
        m = jnp.broadcast_to(small[meta].reshape((1, N_META) + tail), (batch, N_META) + tail)
        return jnp.concatenate([m, real], axis=1)[None]

    def sample_rows(small, tail):
        return small[:n_s].reshape((1, n_seq, dec_seq) + tail)

    kshape = (DIFF_KV_HEADS, 2, DIFF_HEAD_DIM)
    vshape = (DIFF_KV_HEADS, DIFF_V)
    return (y_p.reshape(batch, seq, D_MODEL), y_s.reshape(n_seq, dec_seq, D_MODEL),
            prompt_rows(lat_p, lat_s, (MLA_KV_LORA,)), prompt_rows(kpe_p, kpe_s, (MLA_ROPE,)),
            prompt_rows(kd_p, kd_s, kshape), prompt_rows(vd_p, vd_s, vshape),
            sample_rows(lat_s, (MLA_KV_LORA,)), sample_rows(kpe_s, (MLA_ROPE,)),
            sample_rows(kd_s, kshape), sample_rows(vd_s, vshape))
```

```python
import functools
import math

import jax
import jax.numpy as jnp
from jax import lax
from jax.experimental import pallas as pl
from jax.experimental.pallas import tpu as pltpu

F32 = jnp.float32
BF16 = jnp.bfloat16

D_MODEL = 1024
N_META = 16
EPS = 1e-6
MLA_HEADS = 8
MLA_NOPE = 64
MLA_ROPE = 32
MLA_QK = MLA_NOPE + MLA_ROPE
MLA_V = 64
MLA_Q_LORA = 256
MLA_KV_LORA = 128
MLA_SCALE = 1.0 / math.sqrt(MLA_QK)
ROPE_BASE = 10000.0
DIFF_HEADS = 8
DIFF_KV_HEADS = 4
DIFF_REP = DIFF_HEADS // DIFF_KV_HEADS
DIFF_HEAD_DIM = 64
DIFF_V = 2 * DIFF_HEAD_DIM
DIFF_SCALE = 1.0 / math.sqrt(DIFF_HEAD_DIM)
N_EXPERTS = 32
TOP_K = 4
D_FF = D_MODEL
SWIGLU_LIMIT = 7.0
SWIGLU_ALPHA = 1.702
LOG2E = math.log2(math.e)
LANE = 128
MXU = 256
VMEM_LIMIT = 56 * 1024 * 1024

NEG_INF = float("-inf")
_NT = (((1,), (1,)), ((), ()))


def _dot(a, b):
    return jnp.dot(a, b, preferred_element_type=F32)


def _dot_nt(a, b):
    return lax.dot_general(a, b, _NT, preferred_element_type=F32)


def _rms(x, g):
    return x * lax.rsqrt(jnp.mean(x * x, axis=-1, keepdims=True) + EPS) * g


def _const_spec(shape):
    nd = len(shape)
    return pl.BlockSpec(shape, lambda *_: (0,) * nd)


def _inproj_kernel(x_ref, cq_ref, sq_ref, gattn_ref, wa_ref, gql_ref, wq_ref, wqs_ref, gq_ref, gkv_ref,
                   wuk_ref, gk_ref, wuv_ref, wqd_ref, gdq_ref, wkd_ref, gdk_ref, wvd_ref, wga_ref, wgb_ref,
                   b2_ref,
                   qa_ref, lat_ref, kpe_ref, kmla_ref, vmla_ref, qd_ref, kd_ref, kdb_ref, vd_ref, vdb_ref,
                   sga_ref, sgb_ref):
    xn = _rms(x_ref[...], gattn_ref[...]).astype(BF16)
    cq = cq_ref[...]
    sq = sq_ref[...]
    za = _dot(xn, wa_ref[...])
    lat = _rms(za[:, 256:384], gkv_ref[...])
    lat_ref[...] = lat
    kpe_pad = za[:, 384:512] * cq + za[:, 512:640] * sq
    kpe_ref[...] = kpe_pad[:, MLA_NOPE:MLA_QK]

    qn = _rms(za[:, :MLA_Q_LORA], gql_ref[...]).astype(BF16)
    q = _dot(qn, wq_ref[...])
    qs = _dot(qn, wqs_ref[...])
    gq = gq_ref[...]
    for h in range(MLA_HEADS):
        sl = slice(h * LANE, (h + 1) * LANE)
        blk = q[:, sl] * cq + qs[:, sl] * sq
        ss = jnp.sum(blk * blk, axis=-1, keepdims=True) * (1.0 / MLA_QK)
        qa_ref[:, sl] = (blk * lax.rsqrt(ss + EPS) * gq).astype(BF16)

    latb = lat.astype(BF16)
    kn = _dot(latb, wuk_ref[...])
    gk = gk_ref[...]
    for h in range(MLA_HEADS):
        sl = slice(h * LANE, (h + 1) * LANE)
        blk = kn[:, sl] + kpe_pad
        ss = jnp.sum(blk * blk, axis=-1, keepdims=True) * (1.0 / MLA_QK)
        kmla_ref[:, sl] = (blk * lax.rsqrt(ss + EPS) * gk).astype(BF16)
    vmla_ref[...] = _dot(latb, wuv_ref[...]).astype(BF16)

    b2 = b2_ref[...]
    zq = _dot(xn, wqd_ref[...])
    gdq = gdq_ref[...]
    for j in range(DIFF_HEADS):
        sl = slice(j * LANE, (j + 1) * LANE)
        blk = zq[:, sl]
        ss = _dot((blk * blk).astype(BF16), b2) * (1.0 / DIFF_HEAD_DIM)
        qd_ref[:, sl] = (blk * lax.rsqrt(ss + EPS) * gdq).astype(BF16)
    zk = _dot(xn, wkd_ref[...])
    gdk = gdk_ref[...]
    for j in range(DIFF_KV_HEADS):
        sl = slice(j * LANE, (j + 1) * LANE)
        blk = zk[:, sl]
        ss = _dot((blk * blk).astype(BF16), b2) * (1.0 / DIFF_HEAD_DIM)
        kd = blk * lax.rsqrt(ss + EPS) * gdk
        kd_ref[:, sl] = kd
        kdb_ref[:, sl] = kd.astype(BF16)
    zv = _dot(xn, wvd_ref[...])
    vd_ref[...] = zv
    vdb_ref[...] = zv.astype(BF16)
    sga_ref[...] = jax.nn.sigmoid(_dot(xn, wga_ref[...])).astype(BF16)
    sgb_ref[...] = jax.nn.sigmoid(_dot(xn, wgb_ref[...])).astype(BF16)


def _inproj(x, cq, sq, wts, tm):
    rows = x.shape[0]
    n_tab = cq.shape[0] // tm
    row = lambda i: (i, 0)
    tab = lambda i: (i % n_tab, 0)
    in_specs = [pl.BlockSpec((tm, D_MODEL), row), pl.BlockSpec((tm, LANE), tab), pl.BlockSpec((tm, LANE), tab)]
    in_specs += [_const_spec(w.shape) for w in wts]
    widths = [(8 * LANE, BF16), (MLA_KV_LORA, F32), (MLA_ROPE, F32), (8 * LANE, BF16), (MLA_HEADS * MLA_V, BF16),
              (8 * LANE, BF16), (4 * LANE, F32), (4 * LANE, BF16), (4 * LANE, F32), (4 * LANE, BF16),
              (D_MODEL, BF16), (D_MODEL, BF16)]
    return pl.pallas_call(
        _inproj_kernel,
        grid=(rows // tm,),
        in_specs=in_specs,
        out_specs=[pl.BlockSpec((tm, w), row) for w, _ in widths],
        out_shape=[jax.ShapeDtypeStruct((rows, w), dt) for w, dt in widths],
        compiler_params=pltpu.CompilerParams(dimension_semantics=("parallel",), vmem_limit_bytes=VMEM_LIMIT),
        name="inproj",
    )(x, cq, sq, *wts)


def _flash_init(n, tq):
    return (jnp.full((tq, 1), NEG_INF, F32), jnp.zeros((tq, 2 * LANE), F32)) * n


def _flash_update(s_list, vaug, st):
    out = []
    for i, s in enumerate(s_list):
        m_old, acc = st[2 * i], st[2 * i + 1]
        m_new = jnp.maximum(m_old, jnp.max(s, axis=-1, keepdims=True))
        p = jnp.exp2(s - m_new)
        out += [m_new, jnp.exp2(m_old - m_new) * acc + _dot(p.astype(BF16), vaug)]
    return tuple(out)


def _with_ones(v):
    return jnp.concatenate([v, jnp.ones(v.shape, v.dtype)], axis=1)


def _mla_flash_kernel(q_ref, k_ref, v_ref, km_ref, vm_ref, o_ref, *, tq):
    qi = pl.program_id(2)
    qs = [q_ref[:, h * LANE:(h + 1) * LANE] for h in range(2)]
    col = lax.broadcasted_iota(jnp.int32, (tq, LANE), 1)

    s_meta = [jnp.where(col < N_META, _dot_nt(qs[h], km_ref[:, h * LANE:(h + 1) * LANE]), NEG_INF)
              for h in range(2)]
    st = _flash_update(s_meta, _with_ones(vm_ref[...]), _flash_init(2, tq))

    def tile(j, st, mask):
        start = pl.multiple_of(j * tq, tq)
        s_list = [_dot_nt(qs[h], k_ref[pl.ds(start, tq), h * LANE:(h + 1) * LANE]) for h in range(2)]
        if mask is not None:
            s_list = [jnp.where(mask, s, NEG_INF) for s in s_list]
        return _flash_update(s_list, _with_ones(v_ref[pl.ds(start, tq), :]), st)

    st = lax.fori_loop(0, qi, lambda j, st: tile(j, st, None), st)
    st = tile(qi, st, lax.broadcasted_iota(jnp.int32, (tq, tq), 1) <= lax.broadcasted_iota(jnp.int32, (tq, tq), 0))

    a0, a1 = st[1], st[3]
    o_ref[...] = jnp.where(col < MLA_V, a0[:, :LANE] / a0[:, LANE:], a1[:, :LANE] / a1[:, LANE:]).astype(BF16)


def _mla_flash(qa, kmla, vmla, km, vm, batch, seq, tq):
    nq = seq // tq
    return pl.pallas_call(
        functools.partial(_mla_flash_kernel, tq=tq),
        grid=(batch, MLA_HEADS // 2, nq),
        in_specs=[
            pl.BlockSpec((tq, 2 * LANE), lambda b, hp, i: (b * nq + i, hp)),
            pl.BlockSpec((seq, 2 * LANE), lambda b, hp, i: (b, hp)),
            pl.BlockSpec((seq, LANE), lambda b, hp, i: (b, hp)),
            pl.BlockSpec((LANE, 2 * LANE), lambda b, hp, i: (0, hp)),
            pl.BlockSpec((LANE, LANE), lambda b, hp, i: (0, hp)),
        ],
        out_specs=pl.BlockSpec((tq, LANE), lambda b, hp, i: (b * nq + i, hp)),
        out_shape=jax.ShapeDtypeStruct((batch * seq, MLA_HEADS * MLA_V), BF16),
        compiler_params=pltpu.CompilerParams(dimension_semantics=("parallel", "parallel", "parallel"),
                                             vmem_limit_bytes=VMEM_LIMIT),
        name="mla_flash",
    )(qa, kmla, vmla, km, vm)


def _lambda_value(lq1_ref, lk1_ref, lq2_ref, lk2_ref, lambda_init):
    a = jnp.sum(lq1_ref[...] * lk1_ref[...], axis=-1, keepdims=True)
    b = jnp.sum(lq2_ref[...] * lk2_ref[...], axis=-1, keepdims=True)
    return jnp.exp(a) - jnp.exp(b) + lambda_init


def _diff_flash_kernel(slopes_ref, q_ref, k_ref, v_ref, km_ref, vm_ref, gsub_ref, lq1_ref, lk1_ref, lq2_ref,
                       lk2_ref, o_ref, *, tq, lambda_init):
    g = pl.program_id(1)
    qi = pl.program_id(2)
    lane = lax.broadcasted_iota(jnp.int32, (tq, LANE), 1)
    qs = []
    slopes = []
    for r in range(DIFF_REP):
        qh = q_ref[:, r * LANE:(r + 1) * LANE].astype(F32)
        qs.append(jnp.where(lane < DIFF_HEAD_DIM, qh, 0.0).astype(BF16))
        qs.append(jnp.where(lane >= DIFF_HEAD_DIM, qh, 0.0).astype(BF16))
        slopes += [slopes_ref[g * DIFF_REP + r]] * 2

    q0 = N_META + qi * tq
    kcol = lax.broadcasted_iota(jnp.int32, (1, LANE), 1)
    km = km_ref[...]
    s_meta = [jnp.where(lane < N_META, _dot_nt(qs[i], km) + slopes[i] * (kcol - q0).astype(F32), NEG_INF)
              for i in range(4)]
    st = _flash_update(s_meta, _with_ones(vm_ref[...]), _flash_init(4, tq))

    tcol = lax.broadcasted_iota(jnp.int32, (1, tq), 1)

    def tile(j, st, mask):
        start = pl.multiple_of(j * tq, tq)
        k = k_ref[pl.ds(start, tq), :]
        rel = (tcol + (j - qi) * tq).astype(F32)
        s_list = [_dot_nt(qs[i], k) + slopes[i] * rel for i in range(4)]
        if mask is not None:
            s_list = [jnp.where(mask, s, NEG_INF) for s in s_list]
        return _flash_update(s_list, _with_ones(v_ref[pl.ds(start, tq), :]), st)

    st = lax.fori_loop(0, qi, lambda j, st: tile(j, st, None), st)
    st = tile(qi, st, lax.broadcasted_iota(jnp.int32, (tq, tq), 1) <= lax.broadcasted_iota(jnp.int32, (tq, tq), 0))

    lam = _lambda_value(lq1_ref, lk1_ref, lq2_ref, lk2_ref, lambda_init)
    gsub = gsub_ref[...] * (1.0 - lambda_init)
    for r in range(DIFF_REP):
        a0 = st[4 * r + 1]
        a1 = st[4 * r + 3]
        o = a0[:, :LANE] / a0[:, LANE:] - lam * (a1[:, :LANE] / a1[:, LANE:])
        o_ref[:, r * LANE:(r + 1) * LANE] = _rms(o, gsub).astype(BF16)


def _diff_flash(slopes, qd, kdb, vdb, km, vm, gsub, lams, batch, seq, tq, lambda_init):
    nq = seq // tq
    vec = pl.BlockSpec((1, DIFF_HEAD_DIM), lambda b, g, i, s: (0, 0))
    return pl.pallas_call(
        functools.partial(_diff_flash_kernel, tq=tq, lambda_init=lambda_init),
        grid_spec=pltpu.PrefetchScalarGridSpec(
            num_scalar_prefetch=1,
            grid=(batch, DIFF_KV_HEADS, nq),
            in_specs=[
                pl.BlockSpec((tq, 2 * LANE), lambda b, g, i, s: (b * nq + i, g)),
                pl.BlockSpec((seq, LANE), lambda b, g, i, s: (b, g)),
                pl.BlockSpec((seq, LANE), lambda b, g, i, s: (b, g)),
                pl.BlockSpec((LANE, LANE), lambda b, g, i, s: (0, g)),
                pl.BlockSpec((LANE, LANE), lambda b, g, i, s: (0, g)),
                pl.BlockSpec((1, LANE), lambda b, g, i, s: (0, 0)),
                vec, vec, vec, vec,
            ],
            out_specs=pl.BlockSpec((tq, 2 * LANE), lambda b, g, i, s: (b * nq + i, g)),
        ),
        out_shape=jax.ShapeDtypeStruct((batch * seq, DIFF_HEADS * DIFF_V), BF16),
        compiler_params=pltpu.CompilerParams(dimension_semantics=("parallel", "parallel", "parallel"),
                                             vmem_limit_bytes=VMEM_LIMIT),
        name="diff_flash",
    )(slopes, qd, kdb, vdb, km, vm, gsub, *lams)


def _sample_qprep_kernel(qa_ref, gk_ref, wabs_ref, sel_ref, qabs_ref, qrope_ref):
    gk = gk_ref[...]
    for h in range(MLA_HEADS):
        qg = (qa_ref[:, h * LANE:(h + 1) * LANE].astype(F32) * gk).astype(BF16)
        qabs_ref[:, h * LANE:(h + 1) * LANE] = _dot(qg, wabs_ref[h])
        qrope_ref[h] = _dot(qg, sel_ref[...])


def _sample_qprep(qa_s, gk_pad, wabs, sel):
    rows = qa_s.shape[0]
    return pl.pallas_call(
        _sample_qprep_kernel,
        grid=(1,),
        in_specs=[_const_spec(qa_s.shape), _const_spec((1, LANE)), _const_spec(wabs.shape), _const_spec(sel.shape)],
        out_specs=[_const_spec((rows, MLA_HEADS * LANE)), _const_spec((MLA_HEADS, rows, MLA_ROPE))],
        out_shape=[jax.ShapeDtypeStruct((rows, MLA_HEADS * MLA_KV_LORA), F32),
                   jax.ShapeDtypeStruct((MLA_HEADS, rows, MLA_ROPE), F32)],
        name="sample_qprep",
    )(qa_s, gk_pad, wabs, sel)


def _sample_attn_kernel(ptc_ref, ptn_ref, qabs_ref, qrope_ref, qd_ref, latn_ref, kpen_ref, kdn_ref, vdn_ref,
                        wukt_ref, wuvw_ref, slope_ref, qpos_ref, gsub_ref,
                        lq1_ref, lk1_ref, lq2_ref, lk2_ref, hmask_ref, maska_ref, maskd_ref,
                        clat_ref, ckpe_ref, ckd_ref, cvd_ref,
                        oa_ref, od_ref,
                        lat_buf, kpe_buf, kd_buf, vd_buf, sem, lhs_a, qbd_f, qbd, vnew,
                        ma_ref, la_ref, acca_ref, md_ref, ld_ref, accd_ref,
                        *, n_pages, ppc, page, dec_seq, lambda_init):
    seq_id = pl.program_id(0)
    n_seq = pl.num_programs(0)
    n_chunks = n_pages // ppc
    n_arow = MLA_HEADS * dec_seq
    n_up = MLA_HEADS * MLA_NOPE
    grow = 2 * DIFF_REP * dec_seq
    past_len = n_pages * page

    def copies(pt_ref, c, slot):
        out = []
        for p in range(ppc):
            pg = pt_ref[0, 0, c * ppc + p]
            out.append(pltpu.make_async_copy(clat_ref.at[pg], lat_buf.at[slot, pl.ds(p * page, page)],
                                             sem.at[slot, 0]))
            out.append(pltpu.make_async_copy(ckpe_ref.at[pg], kpe_buf.at[slot, p], sem.at[slot, 1]))
            out.append(pltpu.make_async_copy(ckd_ref.at[pg], kd_buf.at[slot, p], sem.at[slot, 2]))
            out.append(pltpu.make_async_copy(cvd_ref.at[pg], vd_buf.at[slot, p], sem.at[slot, 3]))
        return out

    def issue(pt_ref, c, slot):
        for cp in copies(pt_ref, c, slot):
            cp.start()

    @pl.when(seq_id == 0)
    def _():
        issue(ptc_ref, 0, 0)

    lhs_a[0:n_up, :] = wukt_ref[...]
    lhs_a[n_up:, :] = qabs_ref[...].astype(BF16)
    qrope = qrope_ref[...].astype(BF16)
    qbd_f[...] = jnp.zeros(qbd_f.shape, F32)
    lane = lax.broadcasted_iota(jnp.int32, (dec_seq, LANE), 1)
    for g in range(DIFF_KV_HEADS):
        for c in range(2):
            for r in range(DIFF_REP):
                src = qd_ref[:, (g * DIFF_REP + r) * LANE:(g * DIFF_REP + r + 1) * LANE]
                keep = (lane >= DIFF_HEAD_DIM) if c else (lane < DIFF_HEAD_DIM)
                row0 = g * grow + (c * DIFF_REP + r) * dec_seq
                qbd_f[row0:row0 + dec_seq, g * LANE:(g + 1) * LANE] = jnp.where(keep, src, 0.0)
    qbd[...] = qbd_f[...].astype(BF16)
    vnew[...] = jnp.zeros(vnew.shape, F32)
    vnew[0:DIFF_KV_HEADS * dec_seq, :] = vdn_ref[0]
    ma_ref[...] = jnp.full(ma_ref.shape, NEG_INF, F32)
    la_ref[...] = jnp.zeros(la_ref.shape, F32)
    acca_ref[...] = jnp.zeros(acca_ref.shape, F32)
    md_ref[...] = jnp.full(md_ref.shape, NEG_INF, F32)
    ld_ref[...] = jnp.zeros(ld_ref.shape, F32)
    accd_ref[...] = jnp.zeros(accd_ref.shape, F32)
    slope = slope_ref[...]
    qpos = qpos_ref[...]

    def attend(lat, kpe_pages, kd_pages, v_of, kpos0, mask_a, mask_d):
        npg = len(kpe_pages)
        nk = npg * LANE
        latb = lat.astype(BF16)
        r_all = _dot_nt(lhs_a[...], latb)
        kn = r_all[0:n_up]
        kn2 = jnp.sum((kn * kn).reshape(MLA_HEADS, MLA_NOPE, nk), axis=1)
        kpe = kpe_pages[0] if npg == 1 else jnp.concatenate(kpe_pages, axis=1)
        rp2 = jnp.sum(kpe * kpe, axis=0, keepdims=True)
        rinv = lax.rsqrt((kn2 + rp2) * (1.0 / MLA_QK) + EPS)
        s = r_all[n_up:] + _dot(qrope, kpe.astype(BF16))
        s = (s.reshape(dec_seq, MLA_HEADS, nk) * rinv[None]).reshape(n_arow, nk)
        if mask_a is not None:
            s = jnp.where(mask_a > 0.0, s, NEG_INF)
        m_old = ma_ref[...]
        m_new = jnp.maximum(m_old, jnp.max(s, axis=-1, keepdims=True))
        alpha = jnp.exp2(m_old - m_new)
        p = jnp.exp2(s - m_new)
        la_ref[...] = alpha * la_ref[...] + jnp.sum(p, axis=-1, keepdims=True)
        acca_ref[...] = alpha * acca_ref[...] + _dot(p.astype(BF16), latb)
        ma_ref[...] = m_new

        qb = qbd[...]
        sd = [_dot(qb, kd.astype(BF16)) for kd in kd_pages]
        sd = sd[0] if npg == 1 else jnp.concatenate(sd, axis=1)
        kpos = (kpos0 + lax.broadcasted_iota(jnp.int32, (1, nk), 1)).astype(F32)
        sd = sd - slope * (qpos - kpos)
        if mask_d is not None:
            sd = jnp.where(mask_d > 0.0, sd, NEG_INF)
        m_old = md_ref[...]
        m_new = jnp.maximum(m_old, jnp.max(sd, axis=-1, keepdims=True))
        alpha = jnp.exp2(m_old - m_new)
        p = jnp.exp2(sd - m_new)
        ld_ref[...] = alpha * ld_ref[...] + jnp.sum(p, axis=-1, keepdims=True)
        pb = p.astype(BF16)
        for g in range(DIFF_KV_HEADS):
            rows = slice(g * grow, (g + 1) * grow)
            pv = None
            for pi in range(npg):
                t = _dot(pb[rows, pi * LANE:(pi + 1) * LANE], v_of(pi, g).astype(BF16))
                pv = t if pv is None else pv + t
            accd_ref[rows, :] = alpha[rows] * accd_ref[rows, :] + pv
        md_ref[...] = m_new

    def chunk(c, carry):
        slot = (seq_id * n_chunks + c) % 2

        @pl.when(c + 1 < n_chunks)
        def _():
            issue(ptc_ref, c + 1, 1 - slot)

        @pl.when(jnp.logical_and(c + 1 == n_chunks, seq_id + 1 < n_seq))
        def _():
            issue(ptn_ref, 0, 1 - slot)

        for cp in copies(ptc_ref, c, slot):
            cp.wait()
        attend(lat_buf[slot],
               [kpe_buf[slot, p] for p in range(ppc)],
               [kd_buf[slot, p] for p in range(ppc)],
               lambda p, g: vd_buf[slot, p, pl.ds(g, page, stride=DIFF_KV_HEADS), :],
               c * (ppc * page), None, None)
        return carry

    lax.fori_loop(0, n_chunks, chunk, 0)

    latn = jnp.concatenate([latn_ref[...], jnp.zeros((LANE - dec_seq, MLA_KV_LORA), F32)], axis=0)
    attend(latn, [kpen_ref[0]], [kdn_ref[0]],
           lambda p, g: vnew[pl.ds(g, LANE, stride=DIFF_KV_HEADS), :],
           past_len, maska_ref[...], maskd_ref[...])

    o_lat = (acca_ref[...] / la_ref[...]).astype(BF16)
    res = _dot(o_lat, wuvw_ref[...]) * hmask_ref[...]
    oa_ref[...] = jnp.sum(res.reshape(dec_seq, MLA_HEADS, MLA_HEADS * MLA_V), axis=1)

    lam = _lambda_value(lq1_ref, lk1_ref, lq2_ref, lk2_ref, lambda_init)
    gsub = gsub_ref[...] * (1.0 - lambda_init)
    for g in range(DIFF_KV_HEADS):
        for r in range(DIFF_REP):
            h = g * DIFF_REP + r
            r0 = g * grow + r * dec_seq
            r1 = r0 + DIFF_REP * dec_seq
            o0 = accd_ref[r0:r0 + dec_seq, :] / ld_ref[r0:r0 + dec_seq, :]
            o1 = accd_ref[r1:r1 + dec_seq, :] / ld_ref[r1:r1 + dec_seq, :]
            od_ref[:, h * LANE:(h + 1) * LANE] = _rms(o0 - lam * o1, gsub)


def _sample_attn(page_table, qabs, qrope, qd_s, lat_s, kpen_t, kdn_t, vdn_r, consts, caches, dec_seq, ppc,
                 lambda_init):
    n_seq, n_pages = page_table.shape
    page = caches[0].shape[1]
    ck = ppc * page
    n_arow = MLA_HEADS * dec_seq
    n_drow = 2 * DIFF_HEADS * dec_seq
    kd_rows = DIFF_KV_HEADS * 2 * DIFF_HEAD_DIM
    pt3 = page_table.reshape(n_seq, 1, n_pages)
    per_seq = lambda w: pl.BlockSpec((dec_seq, w), lambda s: (s, 0))
    per_seq3 = lambda a, b: pl.BlockSpec((1, a, b), lambda s: (s, 0, 0))
    any_spec = pl.BlockSpec(memory_space=pl.ANY)
    in_specs = [
        pl.BlockSpec((1, 1, n_pages), lambda s: (s, 0, 0), memory_space=pltpu.SMEM),
        pl.BlockSpec((1, 1, n_pages), lambda s: (jnp.minimum(s + 1, n_seq - 1), 0, 0), memory_space=pltpu.SMEM),
        pl.BlockSpec((n_arow, MLA_KV_LORA), lambda s: (s, 0)),
        pl.BlockSpec((n_arow, MLA_ROPE), lambda s: (s, 0)),
        per_seq(DIFF_HEADS * DIFF_V), per_seq(MLA_KV_LORA),
        per_seq3(MLA_ROPE, LANE), per_seq3(kd_rows, LANE), per_seq3(DIFF_KV_HEADS * dec_seq, DIFF_V),
    ] + [_const_spec(c.shape) for c in consts] + [any_spec] * 4
    return pl.pallas_call(
        functools.partial(_sample_attn_kernel, n_pages=n_pages, ppc=ppc, page=page, dec_seq=dec_seq,
                          lambda_init=lambda_init),
        grid=(n_seq,),
        in_specs=in_specs,
        out_specs=[per_seq(MLA_HEADS * MLA_V), per_seq(DIFF_HEADS * DIFF_V)],
        out_shape=[jax.ShapeDtypeStruct((n_seq * dec_seq, MLA_HEADS * MLA_V), F32),
                   jax.ShapeDtypeStruct((n_seq * dec_seq, DIFF_HEADS * DIFF_V), F32)],
        scratch_shapes=[
            pltpu.VMEM((2, ck, MLA_KV_LORA), F32), pltpu.VMEM((2, ppc, MLA_ROPE, page), F32),
            pltpu.VMEM((2, ppc, kd_rows, page), F32), pltpu.VMEM((2, ppc, DIFF_KV_HEADS * page, DIFF_V), F32),
            pltpu.SemaphoreType.DMA((2, 4)),
            pltpu.VMEM((MLA_HEADS * MLA_NOPE + n_arow, MLA_KV_LORA), BF16),
            pltpu.VMEM((n_drow, kd_rows), F32), pltpu.VMEM((n_drow, kd_rows), BF16),
            pltpu.VMEM((DIFF_KV_HEADS * LANE, DIFF_V), F32),
            pltpu.VMEM((n_arow, 1), F32), pltpu.VMEM((n_arow, 1), F32), pltpu.VMEM((n_arow, MLA_KV_LORA), F32),
            pltpu.VMEM((n_drow, 1), F32), pltpu.VMEM((n_drow, 1), F32), pltpu.VMEM((n_drow, DIFF_V), F32),
        ],
        compiler_params=pltpu.CompilerParams(dimension_semantics=("arbitrary",), vmem_limit_bytes=VMEM_LIMIT),
        name="sample_attn",
    )(pt3, pt3, qabs, qrope, qd_s, lat_s, kpen_t, kdn_t, vdn_r, *consts, *caches)


def _merge_kernel(x_ref, oa_ref, od_ref, sga_ref, sgb_ref, wba_ref, wbb_ref, wo_ref, gffn_ref, wrh_ref, wrl_ref,
                  br_ref, x1_ref, h_ref, gates_ref):
    ya = _dot(oa_ref[...].astype(BF16), wba_ref[...])
    yb = _dot(od_ref[...].astype(BF16), wbb_ref[...])
    m = sga_ref[...].astype(F32) * ya + sgb_ref[...].astype(F32) * yb
    x1 = x_ref[...] + _dot(m.astype(BF16), wo_ref[...])
    x1_ref[...] = x1
    h = _rms(x1, gffn_ref[...])
    hb = h.astype(BF16)
    h_ref[...] = hb
    hl = (h - hb.astype(F32)).astype(BF16)
    logits = _dot(hb, wrh_ref[...]) + _dot(hl, wrh_ref[...]) + _dot(hb, wrl_ref[...]) + br_ref[...]
    iota = lax.broadcasted_iota(jnp.int32, logits.shape, 1).astype(F32)
    vals, hots = [], []
    for _ in range(TOP_K):
        mx = jnp.max(logits, axis=-1, keepdims=True)
        first = jnp.min(jnp.where(logits == mx, iota, float(N_EXPERTS)), axis=-1, keepdims=True)
        hot = iota == first
        vals.append(mx)
        hots.append(hot)
        logits = jnp.where(hot, NEG_INF, logits)
    es = [jnp.exp(v - vals[0]) for v in vals]
    den = es[0] + es[1] + es[2] + es[3]
    gates = jnp.zeros(logits.shape, F32)
    for e, hot in zip(es, hots):
        gates = jnp.where(hot, e / den, gates)
    gates_ref[...] = gates


def _merge(x, oa, od, sga, sgb, wts, tm):
    rows = x.shape[0]
    row = lambda i: (i, 0)
    in_specs = [pl.BlockSpec((tm, D_MODEL), row), pl.BlockSpec((tm, MLA_HEADS * MLA_V), row),
                pl.BlockSpec((tm, DIFF_HEADS * DIFF_V), row), pl.BlockSpec((tm, D_MODEL), row),
                pl.BlockSpec((tm, D_MODEL), row)] + [_const_spec(w.shape) for w in wts]
    return pl.pallas_call(
        _merge_kernel,
        grid=(rows // tm,),
        in_specs=in_specs,
        out_specs=[pl.BlockSpec((tm, D_MODEL), row), pl.BlockSpec((tm, D_MODEL), row),
                   pl.BlockSpec((tm, N_EXPERTS), row)],
        out_shape=[jax.ShapeDtypeStruct((rows, D_MODEL), F32), jax.ShapeDtypeStruct((rows, D_MODEL), BF16),
                   jax.ShapeDtypeStruct((rows, N_EXPERTS), F32)],
        compiler_params=pltpu.CompilerParams(dimension_semantics=("parallel",), vmem_limit_bytes=VMEM_LIMIT),
        name="merge_router",
    )(x, oa, od, sga, sgb, *wts)


def _split_gate_up_kernel(w_ref, perm_ref, wg_ref, wu_ref):
    perm = perm_ref[...]
    for b in range(2 * D_FF // MXU):
        t = _dot(w_ref[0, :, b * MXU:(b + 1) * MXU].astype(BF16), perm)
        wg_ref[0, :, b * LANE:(b + 1) * LANE] = t[:, :LANE].astype(BF16)
        wu_ref[0, :, b * LANE:(b + 1) * LANE] = t[:, LANE:].astype(BF16)


def _split_gate_up(w_gate_up, tk):
    n_e, d_in, d_gu = w_gate_up.shape
    j = jnp.arange(MXU)
    src = jnp.where(j < LANE, 2 * j, 2 * (j - LANE) + 1)
    perm = (jnp.arange(MXU)[:, None] == src[None, :]).astype(BF16)
    blk = lambda e, k: (e, k, 0)
    return pl.pallas_call(
        _split_gate_up_kernel,
        grid=(n_e, d_in // tk),
        in_specs=[pl.BlockSpec((1, tk, d_gu), blk), _const_spec((MXU, MXU))],
        out_specs=[pl.BlockSpec((1, tk, d_gu // 2), blk), pl.BlockSpec((1, tk, d_gu // 2), blk)],
        out_shape=[jax.ShapeDtypeStruct((n_e, d_in, d_gu // 2), BF16)] * 2,
        compiler_params=pltpu.CompilerParams(dimension_semantics=("parallel", "parallel"),
                                             vmem_limit_bytes=VMEM_LIMIT),
        name="split_gate_up",
    )(w_gate_up, perm)


def _moe_kernel(h_ref, g_ref, x1_ref, wg_ref, wu_ref, bg_ref, bu_ref, wd_ref, bd_ref, o_ref):
    e = pl.program_id(1)

    @pl.when(e == 0)
    def _():
        o_ref[...] = x1_ref[...]

    h = h_ref[...]
    glu = jnp.minimum(_dot(h, wg_ref[0]) + bg_ref[0], SWIGLU_LIMIT)
    lin = jnp.clip(_dot(h, wu_ref[0]) + bu_ref[0], -SWIGLU_LIMIT, SWIGLU_LIMIT)
    act = glu * jax.nn.sigmoid(SWIGLU_ALPHA * glu) * (lin + 1.0)
    y = _dot(act.astype(BF16), wd_ref[0]) + bd_ref[0]
    gates = g_ref[...]
    iota = lax.broadcasted_iota(jnp.int32, gates.shape, 1)
    ge = jnp.sum(jnp.where(iota == e, gates, 0.0), axis=-1, keepdims=True)
    o_ref[...] += ge * y


def _moe(h, gates, x1, wg, wu, bg, bu, wd, bd, tm):
    rows = h.shape[0]
    tok = lambda i, e: (i, 0)
    exp3 = lambda i, e: (e, 0, 0)
    return pl.pallas_call(
        _moe_kernel,
        grid=(rows // tm, N_EXPERTS),
        in_specs=[pl.BlockSpec((tm, D_MODEL), tok), pl.BlockSpec((tm, N_EXPERTS), tok),
                  pl.BlockSpec((tm, D_MODEL), tok),
                  pl.BlockSpec((1, D_MODEL, D_FF), exp3), pl.BlockSpec((1, D_MODEL, D_FF), exp3),
                  pl.BlockSpec((1, 1, D_FF), exp3), pl.BlockSpec((1, 1, D_FF), exp3),
                  pl.BlockSpec((1, D_FF, D_MODEL), exp3), pl.BlockSpec((1, 1, D_MODEL), exp3)],
        out_specs=pl.BlockSpec((tm, D_MODEL), tok),
        out_shape=jax.ShapeDtypeStruct((rows, D_MODEL), F32),
        compiler_params=pltpu.CompilerParams(dimension_semantics=("parallel", "arbitrary"),
                                             vmem_limit_bytes=VMEM_LIMIT),
        name="moe",
    )(h, gates, x1, wg, wu, bg, bu, wd, bd)


def _rope_tables(pos):
    inv = 1.0 / (ROPE_BASE ** (jnp.arange(0, MLA_ROPE, 2, dtype=F32) / MLA_ROPE))
    ang = pos.astype(F32)[:, None] * inv[None, :]
    cos, sin = jnp.cos(ang), jnp.sin(ang)
    n = pos.shape[0]
    cq = jnp.concatenate([jnp.ones((n, MLA_NOPE), F32), cos, cos, jnp.zeros((n, LANE - MLA_QK), F32)], axis=1)
    sq = jnp.concatenate([jnp.zeros((n, MLA_NOPE), F32), -sin, sin, jnp.zeros((n, LANE - MLA_QK), F32)], axis=1)
    return cq, sq


def _pad_lanes(w, width=LANE):
    return jnp.pad(w, [(0, 0)] * (w.ndim - 1) + [(0, width - w.shape[-1])])


def _pick_tile(n, cap):
    t = cap
    while n % t:
        t //= 2
    return t


def kernel(x_prompt, x_sample, cache_mla_latent, cache_mla_krope, cache_diff_k, cache_diff_v, page_table, meta_tokens, g_attn, w_in, g_q_lat, w_uq, g_kv_lat, w_uk, w_uv, g_mla_q, g_mla_k, g_diff_q, g_diff_k, lambda_q1, lambda_k1, lambda_q2, lambda_k2, g_subln, w_branch_a, w_branch_b, w_o, g_ffn, w_router, b_router, w_gate_up, b_gate_up, w_down, b_down):
    depth = w_in.shape[0]
    assert depth == 1, "single-layer trunk"
    l = 0
    batch, seq, _ = x_prompt.shape
    n_seq, dec_seq, _ = x_sample.shape
    n_pages = page_table.shape[1]
    n_pool, page = cache_mla_latent.shape[1:3]
    assert page == LANE and dec_seq == 8
    past_len = n_pages * page
    lambda_init = 0.8 - 0.6 * math.exp(-0.3 * l)

    w = w_in[l]
    o = 0
    parts = []
    for width in (MLA_Q_LORA, MLA_KV_LORA, MLA_ROPE, DIFF_HEADS * 2 * DIFF_HEAD_DIM,
                  DIFF_KV_HEADS * 2 * DIFF_HEAD_DIM, DIFF_KV_HEADS * DIFF_V, D_MODEL, D_MODEL):
        parts.append(w[:, o:o + width])
        o += width
    w_ql, w_kvl, w_kpe, w_qd, w_kd, w_vd, w_ga, w_gb = parts
    half = MLA_ROPE // 2
    w_kpe_sw = jnp.concatenate([w_kpe[:, half:], w_kpe[:, :half]], axis=1)
    place = lambda m: jnp.pad(m, ((0, 0), (MLA_NOPE, LANE - MLA_QK)))
    wa = jnp.concatenate([w_ql, w_kvl, place(w_kpe), place(w_kpe_sw)], axis=1).astype(BF16)
    uq = w_uq[l]
    uq_sw = jnp.concatenate([jnp.zeros_like(uq[..., :MLA_NOPE]), uq[..., MLA_NOPE + half:],
                             uq[..., MLA_NOPE:MLA_NOPE + half]], axis=-1)
    wq = _pad_lanes(uq).reshape(MLA_Q_LORA, MLA_HEADS * LANE).astype(BF16)
    wqs = _pad_lanes(uq_sw).reshape(MLA_Q_LORA, MLA_HEADS * LANE).astype(BF16)
    wuk = _pad_lanes(w_uk[l]).reshape(MLA_KV_LORA, MLA_HEADS * LANE).astype(BF16)
    wuv = w_uv[l].reshape(MLA_KV_LORA, MLA_HEADS * MLA_V).astype(BF16)
    gq = _pad_lanes(g_mla_q[l] * (MLA_SCALE * LOG2E))[None]
    gk = _pad_lanes(g_mla_k[l])[None]
    gdq = (jnp.concatenate([g_diff_q[l]] * 2) * (DIFF_SCALE * LOG2E))[None]
    gdk = jnp.concatenate([g_diff_k[l]] * 2)[None]
    grp = jnp.arange(LANE) // DIFF_HEAD_DIM
    b2 = (grp[:, None] == grp[None, :]).astype(BF16)
    in_wts = (g_attn[l][None], wa, g_q_lat[l][None], wq, wqs, gq, g_kv_lat[l][None], wuk, gk, wuv,
              w_qd.astype(BF16), gdq, w_kd.astype(BF16), gdk, w_vd.astype(BF16), w_ga.astype(BF16),
              w_gb.astype(BF16), b2)

    tm_p = _pick_tile(seq, 256)
    cq_p, sq_p = _rope_tables(N_META + jnp.arange(seq, dtype=jnp.int32))
    outs_p = _inproj(x_prompt.reshape(batch * seq, D_MODEL), cq_p, sq_p, in_wts, tm_p)
    (qa_p, lat_p, kpe_p, kmla_p, vmla_p, qd_p, kd_p, kdb_p, vd_p, vdb_p, sga_p, sgb_p) = outs_p

    n_s = n_seq * dec_seq
    tm_s = _pick_tile(n_s, 256)
    n_small = -(-(n_s + N_META) // tm_s) * tm_s
    x_small = jnp.concatenate([x_sample.reshape(n_s, D_MODEL), meta_tokens.astype(F32),
                               jnp.zeros((n_small - n_s - N_META, D_MODEL), F32)], axis=0)
    pos_small = jnp.concatenate([jnp.tile(past_len + jnp.arange(dec_seq, dtype=jnp.int32), n_seq),
                                 jnp.arange(N_META, dtype=jnp.int32),
                                 jnp.zeros((n_small - n_s - N_META,), jnp.int32)])
    cq_s, sq_s = _rope_tables(pos_small)
    outs_s = _inproj(x_small, cq_s, sq_s, in_wts, tm_s)
    (qa_s, lat_s, kpe_s, kmla_s, vmla_s, qd_s, kd_s, kdb_s, vd_s, vdb_s, sga_s, sgb_s) = outs_s
    meta = slice(n_s, n_s + N_META)

    tq = _pick_tile(seq, 512)
    pad_keys = lambda a: jnp.pad(a[meta], ((0, LANE - N_META), (0, 0)))
    o_a_p = _mla_flash(qa_p, kmla_p, vmla_p, pad_keys(kmla_s), pad_keys(vmla_s), batch, seq, tq)
    slopes = jnp.exp2(-8.0 * jnp.arange(1, DIFF_HEADS + 1, dtype=F32) / DIFF_HEADS) * LOG2E
    lams = (lambda_q1[l][None], lambda_k1[l][None], lambda_q2[l][None], lambda_k2[l][None])
    gsub = g_subln[l][None]
    o_d_p = _diff_flash(slopes, qd_p, kdb_p, vdb_p, pad_keys(kdb_s), pad_keys(vdb_s), gsub, lams, batch, seq, tq,
                        lambda_init)

    ukt = jnp.transpose(w_uk[l], (1, 2, 0))
    wabs = jnp.pad(ukt, ((0, 0), (0, LANE - MLA_NOPE), (0, 0))).astype(BF16)
    sel = (jnp.arange(LANE)[:, None] == (MLA_NOPE + jnp.arange(MLA_ROPE))[None, :]).astype(BF16)
    qabs, qrope = _sample_qprep(qa_s[:n_s], gk, wabs, sel)
    n_arow = MLA_HEADS * dec_seq
    qabs = qabs.reshape(n_s * MLA_HEADS, MLA_KV_LORA)
    qrope = jnp.transpose(qrope, (1, 0, 2)).reshape(n_s * MLA_HEADS, MLA_ROPE)
    drow = jnp.arange(2 * DIFF_HEADS * dec_seq)
    d_head = (drow // (2 * DIFF_REP * dec_seq)) * DIFF_REP + (drow // dec_seq) % DIFF_REP
    slope_rows = slopes[d_head][:, None]
    qpos_rows = (past_len + drow % dec_seq).astype(F32)[:, None]
    arow = jnp.arange(n_arow)
    hmask = (arow[:, None] % MLA_HEADS == jnp.arange(MLA_HEADS * MLA_V)[None, :] // MLA_V).astype(F32)
    new_key = jnp.arange(LANE)[None, :]
    mask_a = (new_key <= (arow // MLA_HEADS)[:, None]).astype(F32)
    mask_d = (new_key <= (drow % dec_seq)[:, None]).astype(F32)
    consts = (ukt.reshape(MLA_HEADS * MLA_NOPE, MLA_KV_LORA).astype(BF16), wuv, slope_rows, qpos_rows,
              gsub) + lams + (hmask, mask_a, mask_d)
    kd_rows = DIFF_KV_HEADS * 2 * DIFF_HEAD_DIM
    caches = (cache_mla_latent[l],
              jnp.transpose(cache_mla_krope[l], (0, 2, 1)),
              jnp.transpose(cache_diff_k[l], (0, 2, 3, 4, 1)).reshape(n_pool, kd_rows, page),
              cache_diff_v[l].reshape(n_pool, page * DIFF_KV_HEADS, DIFF_V))
    pad_new = lambda a: jnp.pad(a, ((0, 0), (0, 0), (0, LANE - dec_seq)))
    kpen_t = pad_new(jnp.transpose(kpe_s[:n_s].reshape(n_seq, dec_seq, MLA_ROPE), (0, 2, 1)))
    kdn_t = pad_new(jnp.transpose(kd_s[:n_s].reshape(n_seq, dec_seq, kd_rows), (0, 2, 1)))
    vdn_r = vd_s[:n_s].reshape(n_seq, dec_seq * DIFF_KV_HEADS, DIFF_V)
    ppc = _pick_tile(n_pages, 8)
    o_a_s, o_d_s = _sample_attn(page_table, qabs, qrope, qd_s[:n_s].astype(F32), lat_s[:n_s], kpen_t, kdn_t,
                                vdn_r, consts, caches, dec_seq, ppc, lambda_init)

    wr = w_router[l]
    wr_hi = wr.astype(BF16)
    wr_lo = (wr - wr_hi.astype(F32)).astype(BF16)
    merge_wts = (w_branch_a[l].astype(BF16), w_branch_b[l].astype(BF16), w_o[l].astype(BF16), g_ffn[l][None],
                 wr_hi, wr_lo, b_router[l][None].astype(F32))
    wg, wu = _split_gate_up(w_gate_up[l], 512)
    bg = b_gate_up[l][:, None, 0::2]
    bu = b_gate_up[l][:, None, 1::2]
    wd = w_down[l].astype(BF16)
    bd = b_down[l][:, None, :]

    def channel(x, oa, od, sga, sgb, cap):
        x1, h, gates = _merge(x, oa, od, sga, sgb, merge_wts, _pick_tile(x.shape[0], 256))
        return _moe(h, gates, x1, wg, wu, bg, bu, wd, bd, _pick_tile(x.shape[0], cap))

    y_p = channel(x_prompt.reshape(batch * seq, D_MODEL), o_a_p, o_d_p, sga_p, sgb_p, 512)
    y_s = channel(x_sample.reshape(n_s, D_MODEL), o_a_s, o_d_s, sga_s[:n_s], sgb_s[:n_s], 512)

    def prompt_rows(real, small, tail):
        real = real.reshape((batch, seq) + tail)
        m = jnp.broadcast_to(small[meta].reshape((1, N_META) + tail), (batch, N_META) + tail)
        return jnp.concatenate([m, real], axis=1)[None]

    def sample_rows(small, tail):
        return small[:n_s].reshape((1, n_seq, dec_seq) + tail)

    kshape = (DIFF_KV_HEADS, 2, DIFF_HEAD_DIM)
    vshape = (DIFF_KV_HEADS, DIFF_V)
    return (y_p.reshape(batch, seq, D_MODEL), y_s.reshape(n_seq, dec_seq, D_MODEL),
            prompt_rows(lat_p, lat_s, (MLA_KV_LORA,)), prompt_rows(kpe_p, kpe_s, (MLA_ROPE,)),
            prompt_rows(kd_p, kd_s, kshape), prompt_rows(vd_p, vd_s, vshape),
            sample_rows(lat_s, (MLA_KV_LORA,)), sample_rows(kpe_s, (MLA_ROPE,)),
            sample_rows(kd_s, kshape), sample_rows(vd_s, vshape))
```

```python
import functools
import math

import jax
import jax.numpy as jnp
from jax import lax
from jax.experimental import pallas as pl
from jax.experimental.pallas import tpu as pltpu

F32 = jnp.float32
BF16 = jnp.bfloat16

D_MODEL = 1024
N_META = 16
EPS = 1e-6
MLA_HEADS = 8
MLA_NOPE = 64
MLA_ROPE = 32
MLA_QK = MLA_NOPE + MLA_ROPE
MLA_V = 64
MLA_Q_LORA = 256
MLA_KV_LORA = 128
MLA_SCALE = 1.0 / math.sqrt(MLA_QK)
ROPE_BASE = 10000.0
DIFF_HEADS = 8
DIFF_KV_HEADS = 4
DIFF_REP = DIFF_HEADS // DIFF_KV_HEADS
DIFF_HEAD_DIM = 64
DIFF_V = 2 * DIFF_HEAD_DIM
DIFF_SCALE = 1.0 / math.sqrt(DIFF_HEAD_DIM)
N_EXPERTS = 32
TOP_K = 4
D_FF = D_MODEL
SWIGLU_LIMIT = 7.0
SWIGLU_ALPHA = 1.702
LOG2E = math.log2(math.e)
LANE = 128
MXU = 256
VMEM_LIMIT = 56 * 1024 * 1024

NEG_INF = float("-inf")
_NT = (((1,), (1,)), ((), ()))


def _dot(a, b):
    return jnp.dot(a, b, preferred_element_type=F32)


def _dot_nt(a, b):
    return lax.dot_general(a, b, _NT, preferred_element_type=F32)


def _rms(x, g):
    return x * lax.rsqrt(jnp.mean(x * x, axis=-1, keepdims=True) + EPS) * g


def _const_spec(shape):
    nd = len(shape)
    return pl.BlockSpec(shape, lambda *_: (0,) * nd)


def _inproj_kernel(x_ref, cq_ref, sq_ref, gattn_ref, wa_ref, gql_ref, wq_ref, wqs_ref, gq_ref, gkv_ref,
                   wuk_ref, gk_ref, wuv_ref, wqd_ref, gdq_ref, wkd_ref, gdk_ref, wvd_ref, wga_ref, wgb_ref,
                   b2_ref,
                   qa_ref, lat_ref, kpe_ref, kmla_ref, vmla_ref, qd_ref, kd_ref, kdb_ref, vd_ref, vdb_ref,
                   sga_ref, sgb_ref):
    xn = _rms(x_ref[...], gattn_ref[...]).astype(BF16)
    cq = cq_ref[...]
    sq = sq_ref[...]
    za = _dot(xn, wa_ref[...])
    lat = _rms(za[:, 256:384], gkv_ref[...])
    lat_ref[...] = lat
    kpe_pad = za[:, 384:512] * cq + za[:, 512:640] * sq
    kpe_ref[...] = kpe_pad[:, MLA_NOPE:MLA_QK]

    qn = _rms(za[:, :MLA_Q_LORA], gql_ref[...]).astype(BF16)
    q = _dot(qn, wq_ref[...])
    qs = _dot(qn, wqs_ref[...])
    gq = gq_ref[...]
    for h in range(MLA_HEADS):
        sl = slice(h * LANE, (h + 1) * LANE)
        blk = q[:, sl] * cq + qs[:, sl] * sq
        ss = jnp.sum(blk * blk, axis=-1, keepdims=True) * (1.0 / MLA_QK)
        qa_ref[:, sl] = (blk * lax.rsqrt(ss + EPS) * gq).astype(BF16)

    latb = lat.astype(BF16)
    kn = _dot(latb, wuk_ref[...])
    gk = gk_ref[...]
    for h in range(MLA_HEADS):
        sl = slice(h * LANE, (h + 1) * LANE)
        blk = kn[:, sl] + kpe_pad
        ss = jnp.sum(blk * blk, axis=-1, keepdims=True) * (1.0 / MLA_QK)
        kmla_ref[:, sl] = (blk * lax.rsqrt(ss + EPS) * gk).astype(BF16)
    vmla_ref[...] = _dot(latb, wuv_ref[...]).astype(BF16)

    b2 = b2_ref[...]
    zq = _dot(xn, wqd_ref[...])
    gdq = gdq_ref[...]
    for j in range(DIFF_HEADS):
        sl = slice(j * LANE, (j + 1) * LANE)
        blk = zq[:, sl]
        ss = _dot((blk * blk).astype(BF16), b2) * (1.0 / DIFF_HEAD_DIM)
        qd_ref[:, sl] = (blk * lax.rsqrt(ss + EPS) * gdq).astype(BF16)
    zk = _dot(xn, wkd_ref[...])
    gdk = gdk_ref[...]
    for j in range(DIFF_KV_HEADS):
        sl = slice(j * LANE, (j + 1) * LANE)
        blk = zk[:, sl]
        ss = _dot((blk * blk).astype(BF16), b2) * (1.0 / DIFF_HEAD_DIM)
        kd = blk * lax.rsqrt(ss + EPS) * gdk
        kd_ref[:, sl] = kd
        kdb_ref[:, sl] = kd.astype(BF16)
    zv = _dot(xn, wvd_ref[...])
    vd_ref[...] = zv
    vdb_ref[...] = zv.astype(BF16)
    sga_ref[...] = jax.nn.sigmoid(_dot(xn, wga_ref[...])).astype(BF16)
    sgb_ref[...] = jax.nn.sigmoid(_dot(xn, wgb_ref[...])).astype(BF16)


def _inproj(x, cq, sq, wts, tm):
    rows = x.shape[0]
    n_tab = cq.shape[0] // tm
    row = lambda i: (i, 0)
    tab = lambda i: (i % n_tab, 0)
    in_specs = [pl.BlockSpec((tm, D_MODEL), row), pl.BlockSpec((tm, LANE), tab), pl.BlockSpec((tm, LANE), tab)]
    in_specs += [_const_spec(w.shape) for w in wts]
    widths = [(8 * LANE, BF16), (MLA_KV_LORA, F32), (MLA_ROPE, F32), (8 * LANE, BF16), (MLA_HEADS * MLA_V, BF16),
              (8 * LANE, BF16), (4 * LANE, F32), (4 * LANE, BF16), (4 * LANE, F32), (4 * LANE, BF16),
              (D_MODEL, BF16), (D_MODEL, BF16)]
    return pl.pallas_call(
        _inproj_kernel,
        grid=(rows // tm,),
        in_specs=in_specs,
        out_specs=[pl.BlockSpec((tm, w), row) for w, _ in widths],
        out_shape=[jax.ShapeDtypeStruct((rows, w), dt) for w, dt in widths],
        compiler_params=pltpu.CompilerParams(dimension_semantics=("parallel",), vmem_limit_bytes=VMEM_LIMIT),
        name="inproj",
    )(x, cq, sq, *wts)


def _flash_init(n, tq):
    return (jnp.full((tq, 1), NEG_INF, F32), jnp.zeros((tq, 2 * LANE), F32)) * n


def _flash_update(s_list, vaug, st):
    out = []
    for i, s in enumerate(s_list):
        m_old, acc = st[2 * i], st[2 * i + 1]
        m_new = jnp.maximum(m_old, jnp.max(s, axis=-1, keepdims=True))
        p = jnp.exp2(s - m_new)
        out += [m_new, jnp.exp2(m_old - m_new) * acc + _dot(p.astype(BF16), vaug)]
    return tuple(out)


def _with_ones(v):
    return jnp.concatenate([v, jnp.ones(v.shape, v.dtype)], axis=1)


def _mla_flash_kernel(q_ref, k_ref, v_ref, km_ref, vm_ref, o_ref, *, tq):
    qi = pl.program_id(2)
    qs = [q_ref[:, h * LANE:(h + 1) * LANE] for h in range(2)]
    col = lax.broadcasted_iota(jnp.int32, (tq, LANE), 1)

    s_meta = [jnp.where(col < N_META, _dot_nt(qs[h], km_ref[:, h * LANE:(h + 1) * LANE]), NEG_INF)
              for h in range(2)]
    st = _flash_update(s_meta, _with_ones(vm_ref[...]), _flash_init(2, tq))

    def tile(j, st, mask):
        start = pl.multiple_of(j * tq, tq)
        s_list = [_dot_nt(qs[h], k_ref[pl.ds(start, tq), h * LANE:(h + 1) * LANE]) for h in range(2)]
        if mask is not None:
            s_list = [jnp.where(mask, s, NEG_INF) for s in s_list]
        return _flash_update(s_list, _with_ones(v_ref[pl.ds(start, tq), :]), st)

    st = lax.fori_loop(0, qi, lambda j, st: tile(j, st, None), st)
    st = tile(qi, st, lax.broadcasted_iota(jnp.int32, (tq, tq), 1) <= lax.broadcasted_iota(jnp.int32, (tq, tq), 0))

    a0, a1 = st[1], st[3]
    o_ref[...] = jnp.where(col < MLA_V, a0[:, :LANE] / a0[:, LANE:], a1[:, :LANE] / a1[:, LANE:]).astype(BF16)


def _mla_flash(qa, kmla, vmla, km, vm, batch, seq, tq):
    nq = seq // tq
    return pl.pallas_call(
        functools.partial(_mla_flash_kernel, tq=tq),
        grid=(batch, MLA_HEADS // 2, nq),
        in_specs=[
            pl.BlockSpec((tq, 2 * LANE), lambda b, hp, i: (b * nq + i, hp)),
            pl.BlockSpec((seq, 2 * LANE), lambda b, hp, i: (b, hp)),
            pl.BlockSpec((seq, LANE), lambda b, hp, i: (b, hp)),
            pl.BlockSpec((LANE, 2 * LANE), lambda b, hp, i: (0, hp)),
            pl.BlockSpec((LANE, LANE), lambda b, hp, i: (0, hp)),
        ],
        out_specs=pl.BlockSpec((tq, LANE), lambda b, hp, i: (b * nq + i, hp)),
        out_shape=jax.ShapeDtypeStruct((batch * seq, MLA_HEADS * MLA_V), BF16),
        compiler_params=pltpu.CompilerParams(dimension_semantics=("parallel", "parallel", "parallel"),
                                             vmem_limit_bytes=VMEM_LIMIT),
        name="mla_flash",
    )(qa, kmla, vmla, km, vm)


def _lambda_value(lq1_ref, lk1_ref, lq2_ref, lk2_ref, lambda_init):
    a = jnp.sum(lq1_ref[...] * lk1_ref[...], axis=-1, keepdims=True)
    b = jnp.sum(lq2_ref[...] * lk2_ref[...], axis=-1, keepdims=True)
    return jnp.exp(a) - jnp.exp(b) + lambda_init


def _diff_flash_kernel(slopes_ref, q_ref, k_ref, v_ref, km_ref, vm_ref, gsub_ref, lq1_ref, lk1_ref, lq2_ref,
                       lk2_ref, o_ref, *, tq, lambda_init):
    g = pl.program_id(1)
    qi = pl.program_id(2)
    lane = lax.broadcasted_iota(jnp.int32, (tq, LANE), 1)
    qs = []
    slopes = []
    for r in range(DIFF_REP):
        qh = q_ref[:, r * LANE:(r + 1) * LANE].astype(F32)
        qs.append(jnp.where(lane < DIFF_HEAD_DIM, qh, 0.0).astype(BF16))
        qs.append(jnp.where(lane >= DIFF_HEAD_DIM, qh, 0.0).astype(BF16))
        slopes += [slopes_ref[g * DIFF_REP + r]] * 2

    q0 = N_META + qi * tq
    kcol = lax.broadcasted_iota(jnp.int32, (1, LANE), 1)
    km = km_ref[...]
    s_meta = [jnp.where(lane < N_META, _dot_nt(qs[i], km) + slopes[i] * (kcol - q0).astype(F32), NEG_INF)
              for i in range(4)]
    st = _flash_update(s_meta, _with_ones(vm_ref[...]), _flash_init(4, tq))

    tcol = lax.broadcasted_iota(jnp.int32, (1, tq), 1)

    def tile(j, st, mask):
        start = pl.multiple_of(j * tq, tq)
        k = k_ref[pl.ds(start, tq), :]
        rel = (tcol + (j - qi) * tq).astype(F32)
        s_list = [_dot_nt(qs[i], k) + slopes[i] * rel for i in range(4)]
        if mask is not None:
            s_list = [jnp.where(mask, s, NEG_INF) for s in s_list]
        return _flash_update(s_list, _with_ones(v_ref[pl.ds(start, tq), :]), st)

    st = lax.fori_loop(0, qi, lambda j, st: tile(j, st, None), st)
    st = tile(qi, st, lax.broadcasted_iota(jnp.int32, (tq, tq), 1) <= lax.broadcasted_iota(jnp.int32, (tq, tq), 0))

    lam = _lambda_value(lq1_ref, lk1_ref, lq2_ref, lk2_ref, lambda_init)
    gsub = gsub_ref[...] * (1.0 - lambda_init)
    for r in range(DIFF_REP):
        a0 = st[4 * r + 1]
        a1 = st[4 * r + 3]
        o = a0[:, :LANE] / a0[:, LANE:] - lam * (a1[:, :LANE] / a1[:, LANE:])
        o_ref[:, r * LANE:(r + 1) * LANE] = _rms(o, gsub).astype(BF16)


def _diff_flash(slopes, qd, kdb, vdb, km, vm, gsub, lams, batch, seq, tq, lambda_init):
    nq = seq // tq
    vec = pl.BlockSpec((1, DIFF_HEAD_DIM), lambda b, g, i, s: (0, 0))
    return pl.pallas_call(
        functools.partial(_diff_flash_kernel, tq=tq, lambda_init=lambda_init),
        grid_spec=pltpu.PrefetchScalarGridSpec(
            num_scalar_prefetch=1,
            grid=(batch, DIFF_KV_HEADS, nq),
            in_specs=[
                pl.BlockSpec((tq, 2 * LANE), lambda b, g, i, s: (b * nq + i, g)),
                pl.BlockSpec((seq, LANE), lambda b, g, i, s: (b, g)),
                pl.BlockSpec((seq, LANE), lambda b, g, i, s: (b, g)),
                pl.BlockSpec((LANE, LANE), lambda b, g, i, s: (0, g)),
                pl.BlockSpec((LANE, LANE), lambda b, g, i, s: (0, g)),
                pl.BlockSpec((1, LANE), lambda b, g, i, s: (0, 0)),
                vec, vec, vec, vec,
            ],
            out_specs=pl.BlockSpec((tq, 2 * LANE), lambda b, g, i, s: (b * nq + i, g)),
        ),
        out_shape=jax.ShapeDtypeStruct((batch * seq, DIFF_HEADS * DIFF_V), BF16),
        compiler_params=pltpu.CompilerParams(dimension_semantics=("parallel", "parallel", "parallel"),
                                             vmem_limit_bytes=VMEM_LIMIT),
        name="diff_flash",
    )(slopes, qd, kdb, vdb, km, vm, gsub, *lams)


def _sample_qprep_kernel(qa_ref, gk_ref, wabs_ref, sel_ref, qabs_ref, qrope_ref):
    gk = gk_ref[...]
    for h in range(MLA_HEADS):
        qg = (qa_ref[:, h * LANE:(h + 1) * LANE].astype(F32) * gk).astype(BF16)
        qabs_ref[:, h * LANE:(h + 1) * LANE] = _dot(qg, wabs_ref[h])
        qrope_ref[h] = _dot(qg, sel_ref[...])


def _sample_qprep(qa_s, gk_pad, wabs, sel):
    rows = qa_s.shape[0]
    return pl.pallas_call(
        _sample_qprep_kernel,
        grid=(1,),
        in_specs=[_const_spec(qa_s.shape), _const_spec((1, LANE)), _const_spec(wabs.shape), _const_spec(sel.shape)],
        out_specs=[_const_spec((rows, MLA_HEADS * LANE)), _const_spec((MLA_HEADS, rows, MLA_ROPE))],
        out_shape=[jax.ShapeDtypeStruct((rows, MLA_HEADS * MLA_KV_LORA), F32),
                   jax.ShapeDtypeStruct((MLA_HEADS, rows, MLA_ROPE), F32)],
        name="sample_qprep",
    )(qa_s, gk_pad, wabs, sel)


def _sample_attn_kernel(ptc_ref, ptn_ref, qabs_ref, qrope_ref, qd_ref, latn_ref, kpen_ref, kdn_ref, vdn_ref,
                        wukt_ref, wuvw_ref, slope_ref, qpos_ref, gsub_ref,
                        lq1_ref, lk1_ref, lq2_ref, lk2_ref, hmask_ref, maska_ref, maskd_ref,
                        clat_ref, ckpe_ref, ckd_ref, cvd_ref,
                        oa_ref, od_ref,
                        lat_buf, kpe_buf, kd_buf, vd_buf, sem, lhs_a, qbd_f, qbd, vnew,
                        ma_ref, la_ref, acca_ref, md_ref, ld_ref, accd_ref,
                        *, n_pages, ppc, page, dec_seq, lambda_init):
    seq_id = pl.program_id(0)
    n_seq = pl.num_programs(0)
    n_chunks = n_pages // ppc
    n_arow = MLA_HEADS * dec_seq
    n_up = MLA_HEADS * MLA_NOPE
    grow = 2 * DIFF_REP * dec_seq
    past_len = n_pages * page

    def copies(pt_ref, c, slot):
        out = []
        for p in range(ppc):
            pg = pt_ref[0, 0, c * ppc + p]
            out.append(pltpu.make_async_copy(clat_ref.at[pg], lat_buf.at[slot, pl.ds(p * page, page)],
                                             sem.at[slot, 0]))
            out.append(pltpu.make_async_copy(ckpe_ref.at[pg], kpe_buf.at[slot, p], sem.at[slot, 1]))
            out.append(pltpu.make_async_copy(ckd_ref.at[pg], kd_buf.at[slot, p], sem.at[slot, 2]))
            out.append(pltpu.make_async_copy(cvd_ref.at[pg], vd_buf.at[slot, p], sem.at[slot, 3]))
        return out

    def issue(pt_ref, c, slot):
        for cp in copies(pt_ref, c, slot):
            cp.start()

    @pl.when(seq_id == 0)
    def _():
        issue(ptc_ref, 0, 0)

    lhs_a[0:n_up, :] = wukt_ref[...]
    lhs_a[n_up:, :] = qabs_ref[...].astype(BF16)
    qrope = qrope_ref[...].astype(BF16)
    qbd_f[...] = jnp.zeros(qbd_f.shape, F32)
    lane = lax.broadcasted_iota(jnp.int32, (dec_seq, LANE), 1)
    for g in range(DIFF_KV_HEADS):
        for c in range(2):
            for r in range(DIFF_REP):
                src = qd_ref[:, (g * DIFF_REP + r) * LANE:(g * DIFF_REP + r + 1) * LANE]
                keep = (lane >= DIFF_HEAD_DIM) if c else (lane < DIFF_HEAD_DIM)
                row0 = g * grow + (c * DIFF_REP + r) * dec_seq
                qbd_f[row0:row0 + dec_seq, g * LANE:(g + 1) * LANE] = jnp.where(keep, src, 0.0)
    qbd[...] = qbd_f[...].astype(BF16)
    vnew[...] = jnp.zeros(vnew.shape, F32)
    vnew[0:DIFF_KV_HEADS * dec_seq, :] = vdn_ref[0]
    ma_ref[...] = jnp.full(ma_ref.shape, NEG_INF, F32)
    la_ref[...] = jnp.zeros(la_ref.shape, F32)
    acca_ref[...] = jnp.zeros(acca_ref.shape, F32)
    md_ref[...] = jnp.full(md_ref.shape, NEG_INF, F32)
    ld_ref[...] = jnp.zeros(ld_ref.shape, F32)
    accd_ref[...] = jnp.zeros(accd_ref.shape, F32)
    slope = slope_ref[...]
    qpos = qpos_ref[...]

    def attend(lat, kpe_pages, kd_pages, v_of, kpos0, mask_a, mask_d):
        npg = len(kpe_pages)
        nk = npg * LANE
        latb = lat.astype(BF16)
        r_all = _dot_nt(lhs_a[...], latb)
        kn = r_all[0:n_up]
        kn2 = jnp.sum((kn * kn).reshape(MLA_HEADS, MLA_NOPE, nk), axis=1)
        kpe = kpe_pages[0] if npg == 1 else jnp.concatenate(kpe_pages, axis=1)
        rp2 = jnp.sum(kpe * kpe, axis=0, keepdims=True)
        rinv = lax.rsqrt((kn2 + rp2) * (1.0 / MLA_QK) + EPS)
        s = r_all[n_up:] + _dot(qrope, kpe.astype(BF16))
        s = (s.reshape(dec_seq, MLA_HEADS, nk) * rinv[None]).reshape(n_arow, nk)
        if mask_a is not None:
            s = jnp.where(mask_a > 0.0, s, NEG_INF)
        m_old = ma_ref[...]
        m_new = jnp.maximum(m_old, jnp.max(s, axis=-1, keepdims=True))
        alpha = jnp.exp2(m_old - m_new)
        p = jnp.exp2(s - m_new)
        la_ref[...] = alpha * la_ref[...] + jnp.sum(p, axis=-1, keepdims=True)
        acca_ref[...] = alpha * acca_ref[...] + _dot(p.astype(BF16), latb)
        ma_ref[...] = m_new

        kdb = [kd.astype(BF16) for kd in kd_pages]
        sd = _dot(qbd[...], kdb[0] if npg == 1 else jnp.concatenate(kdb, axis=1))
        kpos = (kpos0 + lax.broadcasted_iota(jnp.int32, (1, nk), 1)).astype(F32)
        sd = sd - slope * (qpos - kpos)
        if mask_d is not None:
            sd = jnp.where(mask_d > 0.0, sd, NEG_INF)
        m_old = md_ref[...]
        m_new = jnp.maximum(m_old, jnp.max(sd, axis=-1, keepdims=True))
        alpha = jnp.exp2(m_old - m_new)
        p = jnp.exp2(sd - m_new)
        ld_ref[...] = alpha * ld_ref[...] + jnp.sum(p, axis=-1, keepdims=True)
        pb = p.astype(BF16)
        for g in range(DIFF_KV_HEADS):
            rows = slice(g * grow, (g + 1) * grow)
            pv = None
            for pi in range(npg):
                t = _dot(pb[rows, pi * LANE:(pi + 1) * LANE], v_of(pi, g).astype(BF16))
                pv = t if pv is None else pv + t
            accd_ref[rows, :] = alpha[rows] * accd_ref[rows, :] + pv
        md_ref[...] = m_new

    def chunk(c, carry):
        slot = (seq_id * n_chunks + c) % 2

        @pl.when(c + 1 < n_chunks)
        def _():
            issue(ptc_ref, c + 1, 1 - slot)

        @pl.when(jnp.logical_and(c + 1 == n_chunks, seq_id + 1 < n_seq))
        def _():
            issue(ptn_ref, 0, 1 - slot)

        for cp in copies(ptc_ref, c, slot):
            cp.wait()
        attend(lat_buf[slot],
               [kpe_buf[slot, p] for p in range(ppc)],
               [kd_buf[slot, p] for p in range(ppc)],
               lambda p, g: vd_buf[slot, p, pl.ds(g, page, stride=DIFF_KV_HEADS), :],
               c * (ppc * page), None, None)
        return carry

    lax.fori_loop(0, n_chunks, chunk, 0)

    latn = jnp.concatenate([latn_ref[...], jnp.zeros((LANE - dec_seq, MLA_KV_LORA), F32)], axis=0)
    attend(latn, [kpen_ref[0]], [kdn_ref[0]],
           lambda p, g: vnew[pl.ds(g, LANE, stride=DIFF_KV_HEADS), :],
           past_len, maska_ref[...], maskd_ref[...])

    o_lat = (acca_ref[...] / la_ref[...]).astype(BF16)
    res = _dot(o_lat, wuvw_ref[...]) * hmask_ref[...]
    oa_ref[...] = jnp.sum(res.reshape(dec_seq, MLA_HEADS, MLA_HEADS * MLA_V), axis=1)

    lam = _lambda_value(lq1_ref, lk1_ref, lq2_ref, lk2_ref, lambda_init)
    gsub = gsub_ref[...] * (1.0 - lambda_init)
    for g in range(DIFF_KV_HEADS):
        for r in range(DIFF_REP):
            h = g * DIFF_REP + r
            r0 = g * grow + r * dec_seq
            r1 = r0 + DIFF_REP * dec_seq
            o0 = accd_ref[r0:r0 + dec_seq, :] / ld_ref[r0:r0 + dec_seq, :]
            o1 = accd_ref[r1:r1 + dec_seq, :] / ld_ref[r1:r1 + dec_seq, :]
            od_ref[:, h * LANE:(h + 1) * LANE] = _rms(o0 - lam * o1, gsub)


def _sample_attn(page_table, qabs, qrope, qd_s, lat_s, kpen_t, kdn_t, vdn_r, consts, caches, dec_seq, ppc,
                 lambda_init):
    n_seq, n_pages = page_table.shape
    page = caches[0].shape[1]
    ck = ppc * page
    n_arow = MLA_HEADS * dec_seq
    n_drow = 2 * DIFF_HEADS * dec_seq
    kd_rows = DIFF_KV_HEADS * 2 * DIFF_HEAD_DIM
    pt3 = page_table.reshape(n_seq, 1, n_pages)
    per_seq = lambda w: pl.BlockSpec((dec_seq, w), lambda s: (s, 0))
    per_seq3 = lambda a, b: pl.BlockSpec((1, a, b), lambda s: (s, 0, 0))
    any_spec = pl.BlockSpec(memory_space=pl.ANY)
    in_specs = [
        pl.BlockSpec((1, 1, n_pages), lambda s: (s, 0, 0), memory_space=pltpu.SMEM),
        pl.BlockSpec((1, 1, n_pages), lambda s: (jnp.minimum(s + 1, n_seq - 1), 0, 0), memory_space=pltpu.SMEM),
        pl.BlockSpec((n_arow, MLA_KV_LORA), lambda s: (s, 0)),
        pl.BlockSpec((n_arow, MLA_ROPE), lambda s: (s, 0)),
        per_seq(DIFF_HEADS * DIFF_V), per_seq(MLA_KV_LORA),
        per_seq3(MLA_ROPE, LANE), per_seq3(kd_rows, LANE), per_seq3(DIFF_KV_HEADS * dec_seq, DIFF_V),
    ] + [_const_spec(c.shape) for c in consts] + [any_spec] * 4
    return pl.pallas_call(
        functools.partial(_sample_attn_kernel, n_pages=n_pages, ppc=ppc, page=page, dec_seq=dec_seq,
                          lambda_init=lambda_init),
        grid=(n_seq,),
        in_specs=in_specs,
        out_specs=[per_seq(MLA_HEADS * MLA_V), per_seq(DIFF_HEADS * DIFF_V)],
        out_shape=[jax.ShapeDtypeStruct((n_seq * dec_seq, MLA_HEADS * MLA_V), F32),
                   jax.ShapeDtypeStruct((n_seq * dec_seq, DIFF_HEADS * DIFF_V), F32)],
        scratch_shapes=[
            pltpu.VMEM((2, ck, MLA_KV_LORA), F32), pltpu.VMEM((2, ppc, MLA_ROPE, page), F32),
            pltpu.VMEM((2, ppc, kd_rows, page), F32), pltpu.VMEM((2, ppc, DIFF_KV_HEADS * page, DIFF_V), F32),
            pltpu.SemaphoreType.DMA((2, 4)),
            pltpu.VMEM((MLA_HEADS * MLA_NOPE + n_arow, MLA_KV_LORA), BF16),
            pltpu.VMEM((n_drow, kd_rows), F32), pltpu.VMEM((n_drow, kd_rows), BF16),
            pltpu.VMEM((DIFF_KV_HEADS * LANE, DIFF_V), F32),
            pltpu.VMEM((n_arow, 1), F32), pltpu.VMEM((n_arow, 1), F32), pltpu.VMEM((n_arow, MLA_KV_LORA), F32),
            pltpu.VMEM((n_drow, 1), F32), pltpu.VMEM((n_drow, 1), F32), pltpu.VMEM((n_drow, DIFF_V), F32),
        ],
        compiler_params=pltpu.CompilerParams(dimension_semantics=("arbitrary",), vmem_limit_bytes=VMEM_LIMIT),
        name="sample_attn",
    )(pt3, pt3, qabs, qrope, qd_s, lat_s, kpen_t, kdn_t, vdn_r, *consts, *caches)


def _merge_kernel(x_ref, oa_ref, od_ref, sga_ref, sgb_ref, wba_ref, wbb_ref, wo_ref, gffn_ref, wrh_ref, wrl_ref,
                  br_ref, x1_ref, h_ref, gates_ref):
    ya = _dot(oa_ref[...].astype(BF16), wba_ref[...])
    yb = _dot(od_ref[...].astype(BF16), wbb_ref[...])
    m = sga_ref[...].astype(F32) * ya + sgb_ref[...].astype(F32) * yb
    x1 = x_ref[...] + _dot(m.astype(BF16), wo_ref[...])
    x1_ref[...] = x1
    h = _rms(x1, gffn_ref[...])
    hb = h.astype(BF16)
    h_ref[...] = hb
    hl = (h - hb.astype(F32)).astype(BF16)
    logits = _dot(hb, wrh_ref[...]) + _dot(hl, wrh_ref[...]) + _dot(hb, wrl_ref[...]) + br_ref[...]
    iota = lax.broadcasted_iota(jnp.int32, logits.shape, 1).astype(F32)
    vals, hots = [], []
    for _ in range(TOP_K):
        mx = jnp.max(logits, axis=-1, keepdims=True)
        first = jnp.min(jnp.where(logits == mx, iota, float(N_EXPERTS)), axis=-1, keepdims=True)
        hot = iota == first
        vals.append(mx)
        hots.append(hot)
        logits = jnp.where(hot, NEG_INF, logits)
    es = [jnp.exp(v - vals[0]) for v in vals]
    den = es[0] + es[1] + es[2] + es[3]
    gates = jnp.zeros(logits.shape, F32)
    for e, hot in zip(es, hots):
        gates = jnp.where(hot, e / den, gates)
    gates_ref[...] = gates


def _merge(x, oa, od, sga, sgb, wts, tm):
    rows = x.shape[0]
    row = lambda i: (i, 0)
    in_specs = [pl.BlockSpec((tm, D_MODEL), row), pl.BlockSpec((tm, MLA_HEADS * MLA_V), row),
                pl.BlockSpec((tm, DIFF_HEADS * DIFF_V), row), pl.BlockSpec((tm, D_MODEL), row),
                pl.BlockSpec((tm, D_MODEL), row)] + [_const_spec(w.shape) for w in wts]
    return pl.pallas_call(
        _merge_kernel,
        grid=(rows // tm,),
        in_specs=in_specs,
        out_specs=[pl.BlockSpec((tm, D_MODEL), row), pl.BlockSpec((tm, D_MODEL), row),
                   pl.BlockSpec((tm, N_EXPERTS), row)],
        out_shape=[jax.ShapeDtypeStruct((rows, D_MODEL), F32), jax.ShapeDtypeStruct((rows, D_MODEL), BF16),
                   jax.ShapeDtypeStruct((rows, N_EXPERTS), F32)],
        compiler_params=pltpu.CompilerParams(dimension_semantics=("parallel",), vmem_limit_bytes=VMEM_LIMIT),
        name="merge_router",
    )(x, oa, od, sga, sgb, *wts)


def _split_gate_up_kernel(w_ref, perm_ref, wg_ref, wu_ref):
    perm = perm_ref[...]
    for b in range(2 * D_FF // MXU):
        t = _dot(w_ref[0, :, b * MXU:(b + 1) * MXU].astype(BF16), perm)
        wg_ref[0, :, b * LANE:(b + 1) * LANE] = t[:, :LANE].astype(BF16)
        wu_ref[0, :, b * LANE:(b + 1) * LANE] = t[:, LANE:].astype(BF16)


def _split_gate_up(w_gate_up, tk):
    n_e, d_in, d_gu = w_gate_up.shape
    j = jnp.arange(MXU)
    src = jnp.where(j < LANE, 2 * j, 2 * (j - LANE) + 1)
    perm = (jnp.arange(MXU)[:, None] == src[None, :]).astype(BF16)
    blk = lambda e, k: (e, k, 0)
    return pl.pallas_call(
        _split_gate_up_kernel,
        grid=(n_e, d_in // tk),
        in_specs=[pl.BlockSpec((1, tk, d_gu), blk), _const_spec((MXU, MXU))],
        out_specs=[pl.BlockSpec((1, tk, d_gu // 2), blk), pl.BlockSpec((1, tk, d_gu // 2), blk)],
        out_shape=[jax.ShapeDtypeStruct((n_e, d_in, d_gu // 2), BF16)] * 2,
        compiler_params=pltpu.CompilerParams(dimension_semantics=("parallel", "parallel"),
                                             vmem_limit_bytes=VMEM_LIMIT),
        name="split_gate_up",
    )(w_gate_up, perm)


def _moe_kernel(cnt_ref, h_ref, g_ref, rc_ref, rr_ref, x1_ref, wg_ref, wu_ref, bg_ref, bu_ref, wd_ref, bd_ref,
                o_ref, *, sub):
    b = pl.program_id(0)
    e = pl.program_id(1)
    blk = h_ref.shape[0]

    @pl.when(e == 0)
    def _():
        o_ref[...] = x1_ref[...]

    count = cnt_ref[b * N_EXPERTS + e]
    lane_e = lax.broadcasted_iota(jnp.int32, (blk, N_EXPERTS), 1) == e
    gate_col = jnp.sum(jnp.where(lane_e, g_ref[...], 0.0), axis=-1, keepdims=True)
    rank_col = jnp.sum(jnp.where(lane_e, rc_ref[...], 0.0), axis=-1, keepdims=True)
    rank_row = rr_ref[pl.ds(e, 1), :]
    row_id = lax.broadcasted_iota(jnp.int32, (sub, blk), 0).astype(F32)
    col_id = lax.broadcasted_iota(jnp.int32, (blk, sub), 1).astype(F32)
    h = h_ref[...]

    def tile(i, carry):
        base = (i * sub).astype(F32)
        pick = jnp.where(rank_row - base == row_id, 1.0, 0.0).astype(BF16)
        xs = _dot(pick, h).astype(BF16)
        glu = jnp.minimum(_dot(xs, wg_ref[0]) + bg_ref[0], SWIGLU_LIMIT)
        lin = jnp.clip(_dot(xs, wu_ref[0]) + bu_ref[0], -SWIGLU_LIMIT, SWIGLU_LIMIT)
        act = glu * jax.nn.sigmoid(SWIGLU_ALPHA * glu) * (lin + 1.0)
        y = _dot(act.astype(BF16), wd_ref[0]) + bd_ref[0]
        y_hi = y.astype(BF16)
        y_lo = (y - y_hi.astype(F32)).astype(BF16)
        spread = jnp.where(rank_col - base == col_id, 1.0, 0.0).astype(BF16)
        o_ref[...] += gate_col * (_dot(spread, y_hi) + _dot(spread, y_lo))
        return carry

    lax.fori_loop(0, (count + sub - 1) // sub, tile, 0)


def _moe(h, gates, x1, wg, wu, bg, bu, wd, bd, blk):
    rows = h.shape[0]
    n_blk = rows // blk
    sub = min(3 * blk // 16, blk) if blk >= MXU else blk
    r3 = (gates > 0.0).reshape(n_blk, blk, N_EXPERTS)
    rank = jnp.where(r3, jnp.cumsum(r3.astype(jnp.int32), axis=1) - 1, -1).astype(F32).reshape(rows, N_EXPERTS)
    counts = jnp.sum(r3.astype(jnp.int32), axis=1).reshape(n_blk * N_EXPERTS)
    tok = lambda i, e, c: (i, 0)
    exp3 = lambda i, e, c: (e, 0, 0)
    return pl.pallas_call(
        functools.partial(_moe_kernel, sub=sub),
        grid_spec=pltpu.PrefetchScalarGridSpec(
            num_scalar_prefetch=1,
            grid=(n_blk, N_EXPERTS),
            in_specs=[pl.BlockSpec((blk, D_MODEL), tok), pl.BlockSpec((blk, N_EXPERTS), tok),
                      pl.BlockSpec((blk, N_EXPERTS), tok),
                      pl.BlockSpec((N_EXPERTS, blk), lambda i, e, c: (0, i)),
                      pl.BlockSpec((blk, D_MODEL), tok),
                      pl.BlockSpec((1, D_MODEL, D_FF), exp3), pl.BlockSpec((1, D_MODEL, D_FF), exp3),
                      pl.BlockSpec((1, 1, D_FF), exp3), pl.BlockSpec((1, 1, D_FF), exp3),
                      pl.BlockSpec((1, D_FF, D_MODEL), exp3), pl.BlockSpec((1, 1, D_MODEL), exp3)],
            out_specs=pl.BlockSpec((blk, D_MODEL), tok),
        ),
        out_shape=jax.ShapeDtypeStruct((rows, D_MODEL), F32),
        compiler_params=pltpu.CompilerParams(dimension_semantics=("parallel", "arbitrary"),
                                             vmem_limit_bytes=VMEM_LIMIT),
        name="moe",
    )(counts, h, gates, rank, rank.T, x1, wg, wu, bg, bu, wd, bd)


def _rope_tables(pos):
    inv = 1.0 / (ROPE_BASE ** (jnp.arange(0, MLA_ROPE, 2, dtype=F32) / MLA_ROPE))
    ang = pos.astype(F32)[:, None] * inv[None, :]
    cos, sin = jnp.cos(ang), jnp.sin(ang)
    n = pos.shape[0]
    cq = jnp.concatenate([jnp.ones((n, MLA_NOPE), F32), cos, cos, jnp.zeros((n, LANE - MLA_QK), F32)], axis=1)
    sq = jnp.concatenate([jnp.zeros((n, MLA_NOPE), F32), -sin, sin, jnp.zeros((n, LANE - MLA_QK), F32)], axis=1)
    return cq, sq


def _pad_lanes(w, width=LANE):
    return jnp.pad(w, [(0, 0)] * (w.ndim - 1) + [(0, width - w.shape[-1])])


def _pick_tile(n, cap):
    t = cap
    while n % t:
        t //= 2
    return t


def kernel(x_prompt, x_sample, cache_mla_latent, cache_mla_krope, cache_diff_k, cache_diff_v, page_table, meta_tokens, g_attn, w_in, g_q_lat, w_uq, g_kv_lat, w_uk, w_uv, g_mla_q, g_mla_k, g_diff_q, g_diff_k, lambda_q1, lambda_k1, lambda_q2, lambda_k2, g_subln, w_branch_a, w_branch_b, w_o, g_ffn, w_router, b_router, w_gate_up, b_gate_up, w_down, b_down):
    depth = w_in.shape[0]
    assert depth == 1, "single-layer trunk"
    l = 0
    batch, seq, _ = x_prompt.shape
    n_seq, dec_seq, _ = x_sample.shape
    n_pages = page_table.shape[1]
    n_pool, page = cache_mla_latent.shape[1:3]
    assert page == LANE and dec_seq == 8
    past_len = n_pages * page
    lambda_init = 0.8 - 0.6 * math.exp(-0.3 * l)

    w = w_in[l]
    o = 0
    parts = []
    for width in (MLA_Q_LORA, MLA_KV_LORA, MLA_ROPE, DIFF_HEADS * 2 * DIFF_HEAD_DIM,
                  DIFF_KV_HEADS * 2 * DIFF_HEAD_DIM, DIFF_KV_HEADS * DIFF_V, D_MODEL, D_MODEL):
        parts.append(w[:, o:o + width])
        o += width
    w_ql, w_kvl, w_kpe, w_qd, w_kd, w_vd, w_ga, w_gb = parts
    half = MLA_ROPE // 2
    w_kpe_sw = jnp.concatenate([w_kpe[:, half:], w_kpe[:, :half]], axis=1)
    place = lambda m: jnp.pad(m, ((0, 0), (MLA_NOPE, LANE - MLA_QK)))
    wa = jnp.concatenate([w_ql, w_kvl, place(w_kpe), place(w_kpe_sw)], axis=1).astype(BF16)
    uq = w_uq[l]
    uq_sw = jnp.concatenate([jnp.zeros_like(uq[..., :MLA_NOPE]), uq[..., MLA_NOPE + half:],
                             uq[..., MLA_NOPE:MLA_NOPE + half]], axis=-1)
    wq = _pad_lanes(uq).reshape(MLA_Q_LORA, MLA_HEADS * LANE).astype(BF16)
    wqs = _pad_lanes(uq_sw).reshape(MLA_Q_LORA, MLA_HEADS * LANE).astype(BF16)
    wuk = _pad_lanes(w_uk[l]).reshape(MLA_KV_LORA, MLA_HEADS * LANE).astype(BF16)
    wuv = w_uv[l].reshape(MLA_KV_LORA, MLA_HEADS * MLA_V).astype(BF16)
    gq = _pad_lanes(g_mla_q[l] * (MLA_SCALE * LOG2E))[None]
    gk = _pad_lanes(g_mla_k[l])[None]
    gdq = (jnp.concatenate([g_diff_q[l]] * 2) * (DIFF_SCALE * LOG2E))[None]
    gdk = jnp.concatenate([g_diff_k[l]] * 2)[None]
    grp = jnp.arange(LANE) // DIFF_HEAD_DIM
    b2 = (grp[:, None] == grp[None, :]).astype(BF16)
    in_wts = (g_attn[l][None], wa, g_q_lat[l][None], wq, wqs, gq, g_kv_lat[l][None], wuk, gk, wuv,
              w_qd.astype(BF16), gdq, w_kd.astype(BF16), gdk, w_vd.astype(BF16), w_ga.astype(BF16),
              w_gb.astype(BF16), b2)

    tm_p = _pick_tile(seq, 256)
    cq_p, sq_p = _rope_tables(N_META + jnp.arange(seq, dtype=jnp.int32))
    outs_p = _inproj(x_prompt.reshape(batch * seq, D_MODEL), cq_p, sq_p, in_wts, tm_p)
    (qa_p, lat_p, kpe_p, kmla_p, vmla_p, qd_p, kd_p, kdb_p, vd_p, vdb_p, sga_p, sgb_p) = outs_p

    n_s = n_seq * dec_seq
    tm_s = _pick_tile(n_s, 256)
    n_small = -(-(n_s + N_META) // tm_s) * tm_s
    x_small = jnp.concatenate([x_sample.reshape(n_s, D_MODEL), meta_tokens.astype(F32),
                               jnp.zeros((n_small - n_s - N_META, D_MODEL), F32)], axis=0)
    pos_small = jnp.concatenate([jnp.tile(past_len + jnp.arange(dec_seq, dtype=jnp.int32), n_seq),
                                 jnp.arange(N_META, dtype=jnp.int32),
                                 jnp.zeros((n_small - n_s - N_META,), jnp.int32)])
    cq_s, sq_s = _rope_tables(pos_small)
    outs_s = _inproj(x_small, cq_s, sq_s, in_wts, tm_s)
    (qa_s, lat_s, kpe_s, kmla_s, vmla_s, qd_s, kd_s, kdb_s, vd_s, vdb_s, sga_s, sgb_s) = outs_s
    meta = slice(n_s, n_s + N_META)

    tq = _pick_tile(seq, 1024)
    pad_keys = lambda a: jnp.pad(a[meta], ((0, LANE - N_META), (0, 0)))
    o_a_p = _mla_flash(qa_p, kmla_p, vmla_p, pad_keys(kmla_s), pad_keys(vmla_s), batch, seq, tq)
    slopes = jnp.exp2(-8.0 * jnp.arange(1, DIFF_HEADS + 1, dtype=F32) / DIFF_HEADS) * LOG2E
    lams = (lambda_q1[l][None], lambda_k1[l][None], lambda_q2[l][None], lambda_k2[l][None])
    gsub = g_subln[l][None]
    o_d_p = _diff_flash(slopes, qd_p, kdb_p, vdb_p, pad_keys(kdb_s), pad_keys(vdb_s), gsub, lams, batch, seq, tq,
                        lambda_init)

    ukt = jnp.transpose(w_uk[l], (1, 2, 0))
    wabs = jnp.pad(ukt, ((0, 0), (0, LANE - MLA_NOPE), (0, 0))).astype(BF16)
    sel = (jnp.arange(LANE)[:, None] == (MLA_NOPE + jnp.arange(MLA_ROPE))[None, :]).astype(BF16)
    qabs, qrope = _sample_qprep(qa_s[:n_s], gk, wabs, sel)
    n_arow = MLA_HEADS * dec_seq
    qabs = qabs.reshape(n_s * MLA_HEADS, MLA_KV_LORA)
    qrope = jnp.transpose(qrope, (1, 0, 2)).reshape(n_s * MLA_HEADS, MLA_ROPE)
    drow = jnp.arange(2 * DIFF_HEADS * dec_seq)
    d_head = (drow // (2 * DIFF_REP * dec_seq)) * DIFF_REP + (drow // dec_seq) % DIFF_REP
    slope_rows = slopes[d_head][:, None]
    qpos_rows = (past_len + drow % dec_seq).astype(F32)[:, None]
    arow = jnp.arange(n_arow)
    hmask = (arow[:, None] % MLA_HEADS == jnp.arange(MLA_HEADS * MLA_V)[None, :] // MLA_V).astype(F32)
    new_key = jnp.arange(LANE)[None, :]
    mask_a = (new_key <= (arow // MLA_HEADS)[:, None]).astype(F32)
    mask_d = (new_key <= (drow % dec_seq)[:, None]).astype(F32)
    consts = (ukt.reshape(MLA_HEADS * MLA_NOPE, MLA_KV_LORA).astype(BF16), wuv, slope_rows, qpos_rows,
              gsub) + lams + (hmask, mask_a, mask_d)
    kd_rows = DIFF_KV_HEADS * 2 * DIFF_HEAD_DIM
    caches = (cache_mla_latent[l],
              jnp.transpose(cache_mla_krope[l], (0, 2, 1)),
              jnp.transpose(cache_diff_k[l], (0, 2, 3, 4, 1)).reshape(n_pool, kd_rows, page),
              cache_diff_v[l].reshape(n_pool, page * DIFF_KV_HEADS, DIFF_V))
    pad_new = lambda a: jnp.pad(a, ((0, 0), (0, 0), (0, LANE - dec_seq)))
    kpen_t = pad_new(jnp.transpose(kpe_s[:n_s].reshape(n_seq, dec_seq, MLA_ROPE), (0, 2, 1)))
    kdn_t = pad_new(jnp.transpose(kd_s[:n_s].reshape(n_seq, dec_seq, kd_rows), (0, 2, 1)))
    vdn_r = vd_s[:n_s].reshape(n_seq, dec_seq * DIFF_KV_HEADS, DIFF_V)
    ppc = _pick_tile(n_pages, 16)
    o_a_s, o_d_s = _sample_attn(page_table, qabs, qrope, qd_s[:n_s].astype(F32), lat_s[:n_s], kpen_t, kdn_t,
                                vdn_r, consts, caches, dec_seq, ppc, lambda_init)

    wr = w_router[l]
    wr_hi = wr.astype(BF16)
    wr_lo = (wr - wr_hi.astype(F32)).astype(BF16)
    merge_wts = (w_branch_a[l].astype(BF16), w_branch_b[l].astype(BF16), w_o[l].astype(BF16), g_ffn[l][None],
                 wr_hi, wr_lo, b_router[l][None].astype(F32))
    wg, wu = _split_gate_up(w_gate_up[l], 512)
    bg = b_gate_up[l][:, None, 0::2]
    bu = b_gate_up[l][:, None, 1::2]
    wd = w_down[l].astype(BF16)
    bd = b_down[l][:, None, :]

    def channel(x, oa, od, sga, sgb, cap):
        x1, h, gates = _merge(x, oa, od, sga, sgb, merge_wts, _pick_tile(x.shape[0], 256))
        return _moe(h, gates, x1, wg, wu, bg, bu, wd, bd, _pick_tile(x.shape[0], cap))

    y_p = channel(x_prompt.reshape(batch * seq, D_MODEL), o_a_p, o_d_p, sga_p, sgb_p, 1024)
    y_s = channel(x_sample.reshape(n_s, D_MODEL), o_a_s, o_d_s, sga_s[:n_s], sgb_s[:n_s], 1024)

    def prompt_rows(real, small, tail):
        real = real.reshape((batch, seq) + tail)
        m = jnp.broadcast_to(small[meta].reshape((1, N_META) + tail), (batch, N_META) + tail)
        return jnp.concatenate([m, real], axis=1)[None]

    def sample_rows(small, tail):
        return small[:n_s].reshape((1, n_seq, dec_seq) + tail)

    kshape = (DIFF_KV_HEADS, 2, DIFF_HEAD_DIM)
    vshape = (DIFF_KV_HEADS, DIFF_V)
    return (y_p.reshape(batch, seq, D_MODEL), y_s.reshape(n_seq, dec_seq, D_MODEL),
            prompt_rows(lat_p, lat_s, (MLA_KV_LORA,)), prompt_rows(kpe_p, kpe_s, (MLA_ROPE,)),
            prompt_rows(kd_p, kd_s, kshape), prompt_rows(vd_p, vd_s, vshape),
            sample_rows(lat_s, (MLA_KV_LORA,)), sample_rows(kpe_s, (MLA_ROPE,)),
            sample_rows(kd_s, kshape), sample_rows(vd_s, vshape))
```

```python
import functools
import math

import jax
import jax.numpy as jnp
from jax import lax
from jax.experimental import pallas as pl
from jax.experimental.pallas import tpu as pltpu

F32 = jnp.float32
BF16 = jnp.bfloat16

D_MODEL = 1024
N_META = 16
EPS = 1e-6
MLA_HEADS = 8
MLA_NOPE = 64
MLA_ROPE = 32
MLA_QK = MLA_NOPE + MLA_ROPE
MLA_V = 64
MLA_Q_LORA = 256
MLA_KV_LORA = 128
MLA_SCALE = 1.0 / math.sqrt(MLA_QK)
ROPE_BASE = 10000.0
DIFF_HEADS = 8
DIFF_KV_HEADS = 4
DIFF_REP = DIFF_HEADS // DIFF_KV_HEADS
DIFF_HEAD_DIM = 64
DIFF_V = 2 * DIFF_HEAD_DIM
DIFF_SCALE = 1.0 / math.sqrt(DIFF_HEAD_DIM)
N_EXPERTS = 32
TOP_K = 4
D_FF = D_MODEL
SWIGLU_LIMIT = 7.0
SWIGLU_ALPHA = 1.702
LOG2E = math.log2(math.e)
LANE = 128
MXU = 256
VMEM_LIMIT = 56 * 1024 * 1024
N_SLOTS = 3

NEG_INF = float("-inf")
_NT = (((1,), (1,)), ((), ()))


def _dot(a, b):
    return jnp.dot(a, b, preferred_element_type=F32)


def _dot_nt(a, b):
    return lax.dot_general(a, b, _NT, preferred_element_type=F32)


def _rms(x, g):
    return x * lax.rsqrt(jnp.mean(x * x, axis=-1, keepdims=True) + EPS) * g


def _const_spec(shape):
    nd = len(shape)
    return pl.BlockSpec(shape, lambda *_: (0,) * nd)


def _inproj_kernel(x_ref, cq_ref, sq_ref, gattn_ref, wa_ref, gql_ref, wq_ref, wqs_ref, gq_ref, gkv_ref,
                   wuk_ref, gk_ref, wuv_ref, wqd_ref, gdq_ref, wkd_ref, gdk_ref, wvd_ref, wga_ref, wgb_ref,
                   b2_ref,
                   qa_ref, lat_ref, kpe_ref, kmla_ref, vmla_ref, qd_ref, kd_ref, kdb_ref, vd_ref, vdb_ref,
                   sga_ref, sgb_ref):
    xn = _rms(x_ref[...], gattn_ref[...]).astype(BF16)
    cq = cq_ref[...]
    sq = sq_ref[...]
    za = _dot(xn, wa_ref[...])
    lat = _rms(za[:, 256:384], gkv_ref[...])
    lat_ref[...] = lat
    kpe_pad = za[:, 384:512] * cq + za[:, 512:640] * sq
    kpe_ref[...] = kpe_pad[:, MLA_NOPE:MLA_QK]

    qn = _rms(za[:, :MLA_Q_LORA], gql_ref[...]).astype(BF16)
    q = _dot(qn, wq_ref[...])
    qs = _dot(qn, wqs_ref[...])
    gq = gq_ref[...]
    for h in range(MLA_HEADS):
        sl = slice(h * LANE, (h + 1) * LANE)
        blk = q[:, sl] * cq + qs[:, sl] * sq
        ss = jnp.sum(blk * blk, axis=-1, keepdims=True) * (1.0 / MLA_QK)
        qa_ref[:, sl] = (blk * lax.rsqrt(ss + EPS) * gq).astype(BF16)

    latb = lat.astype(BF16)
    kn = _dot(latb, wuk_ref[...])
    gk = gk_ref[...]
    for h in range(MLA_HEADS):
        sl = slice(h * LANE, (h + 1) * LANE)
        blk = kn[:, sl] + kpe_pad
        ss = jnp.sum(blk * blk, axis=-1, keepdims=True) * (1.0 / MLA_QK)
        kmla_ref[:, sl] = (blk * lax.rsqrt(ss + EPS) * gk).astype(BF16)
    vmla_ref[...] = _dot(latb, wuv_ref[...]).astype(BF16)

    b2 = b2_ref[...]
    zq = _dot(xn, wqd_ref[...])
    gdq = gdq_ref[...]
    for j in range(DIFF_HEADS):
        sl = slice(j * LANE, (j + 1) * LANE)
        blk = zq[:, sl]
        ss = _dot((blk * blk).astype(BF16), b2) * (1.0 / DIFF_HEAD_DIM)
        qd_ref[:, sl] = (blk * lax.rsqrt(ss + EPS) * gdq).astype(BF16)
    zk = _dot(xn, wkd_ref[...])
    gdk = gdk_ref[...]
    for j in range(DIFF_KV_HEADS):
        sl = slice(j * LANE, (j + 1) * LANE)
        blk = zk[:, sl]
        ss = _dot((blk * blk).astype(BF16), b2) * (1.0 / DIFF_HEAD_DIM)
        kd = blk * lax.rsqrt(ss + EPS) * gdk
        kd_ref[:, sl] = kd
        kdb_ref[:, sl] = kd.astype(BF16)
    zv = _dot(xn, wvd_ref[...])
    vd_ref[...] = zv
    vdb_ref[...] = zv.astype(BF16)
    sga_ref[...] = jax.nn.sigmoid(_dot(xn, wga_ref[...])).astype(BF16)
    sgb_ref[...] = jax.nn.sigmoid(_dot(xn, wgb_ref[...])).astype(BF16)


def _inproj(x, cq, sq, wts, tm):
    rows = x.shape[0]
    n_tab = cq.shape[0] // tm
    row = lambda i: (i, 0)
    tab = lambda i: (i % n_tab, 0)
    in_specs = [pl.BlockSpec((tm, D_MODEL), row), pl.BlockSpec((tm, LANE), tab), pl.BlockSpec((tm, LANE), tab)]
    in_specs += [_const_spec(w.shape) for w in wts]
    widths = [(8 * LANE, BF16), (MLA_KV_LORA, F32), (MLA_ROPE, F32), (8 * LANE, BF16), (MLA_HEADS * MLA_V, BF16),
              (8 * LANE, BF16), (4 * LANE, F32), (4 * LANE, BF16), (4 * LANE, F32), (4 * LANE, BF16),
              (D_MODEL, BF16), (D_MODEL, BF16)]
    return pl.pallas_call(
        _inproj_kernel,
        grid=(rows // tm,),
        in_specs=in_specs,
        out_specs=[pl.BlockSpec((tm, w), row) for w, _ in widths],
        out_shape=[jax.ShapeDtypeStruct((rows, w), dt) for w, dt in widths],
        compiler_params=pltpu.CompilerParams(dimension_semantics=("parallel",), vmem_limit_bytes=VMEM_LIMIT),
        name="inproj",
    )(x, cq, sq, *wts)


def _flash_init(n, tq):
    return (jnp.full((tq, 1), NEG_INF, F32), jnp.zeros((tq, 2 * LANE), F32)) * n


def _flash_update(s_list, vaug, st):
    out = []
    for i, s in enumerate(s_list):
        m_old, acc = st[2 * i], st[2 * i + 1]
        m_new = jnp.maximum(m_old, jnp.max(s, axis=-1, keepdims=True))
        p = jnp.exp2(s - m_new)
        out += [m_new, jnp.exp2(m_old - m_new) * acc + _dot(p.astype(BF16), vaug)]
    return tuple(out)


def _with_ones(v):
    return jnp.concatenate([v, jnp.ones(v.shape, v.dtype)], axis=1)


def _mla_flash_kernel(q_ref, k_ref, v_ref, km_ref, vm_ref, o_ref, *, tq):
    qi = pl.program_id(2)
    qs = [q_ref[:, h * LANE:(h + 1) * LANE] for h in range(2)]
    col = lax.broadcasted_iota(jnp.int32, (tq, LANE), 1)

    s_meta = [jnp.where(col < N_META, _dot_nt(qs[h], km_ref[:, h * LANE:(h + 1) * LANE]), NEG_INF)
              for h in range(2)]
    st = _flash_update(s_meta, _with_ones(vm_ref[...]), _flash_init(2, tq))

    def tile(j, st, mask):
        start = pl.multiple_of(j * tq, tq)
        s_list = [_dot_nt(qs[h], k_ref[pl.ds(start, tq), h * LANE:(h + 1) * LANE]) for h in range(2)]
        if mask is not None:
            s_list = [jnp.where(mask, s, NEG_INF) for s in s_list]
        return _flash_update(s_list, _with_ones(v_ref[pl.ds(start, tq), :]), st)

    st = lax.fori_loop(0, qi, lambda j, st: tile(j, st, None), st)
    st = tile(qi, st, lax.broadcasted_iota(jnp.int32, (tq, tq), 1) <= lax.broadcasted_iota(jnp.int32, (tq, tq), 0))

    a0, a1 = st[1], st[3]
    o_ref[...] = jnp.where(col < MLA_V, a0[:, :LANE] / a0[:, LANE:], a1[:, :LANE] / a1[:, LANE:]).astype(BF16)


def _mla_flash(qa, kmla, vmla, km, vm, batch, seq, tq):
    nq = seq // tq
    return pl.pallas_call(
        functools.partial(_mla_flash_kernel, tq=tq),
        grid=(batch, MLA_HEADS // 2, nq),
        in_specs=[
            pl.BlockSpec((tq, 2 * LANE), lambda b, hp, i: (b * nq + i, hp)),
            pl.BlockSpec((seq, 2 * LANE), lambda b, hp, i: (b, hp)),
            pl.BlockSpec((seq, LANE), lambda b, hp, i: (b, hp)),
            pl.BlockSpec((LANE, 2 * LANE), lambda b, hp, i: (0, hp)),
            pl.BlockSpec((LANE, LANE), lambda b, hp, i: (0, hp)),
        ],
        out_specs=pl.BlockSpec((tq, LANE), lambda b, hp, i: (b * nq + i, hp)),
        out_shape=jax.ShapeDtypeStruct((batch * seq, MLA_HEADS * MLA_V), BF16),
        compiler_params=pltpu.CompilerParams(dimension_semantics=("parallel", "parallel", "parallel"),
                                             vmem_limit_bytes=VMEM_LIMIT),
        name="mla_flash",
    )(qa, kmla, vmla, km, vm)


def _lambda_value(lq1_ref, lk1_ref, lq2_ref, lk2_ref, lambda_init):
    a = jnp.sum(lq1_ref[...] * lk1_ref[...], axis=-1, keepdims=True)
    b = jnp.sum(lq2_ref[...] * lk2_ref[...], axis=-1, keepdims=True)
    return jnp.exp(a) - jnp.exp(b) + lambda_init


def _diff_flash_kernel(slopes_ref, q_ref, k_ref, v_ref, km_ref, vm_ref, gsub_ref, lq1_ref, lk1_ref, lq2_ref,
                       lk2_ref, o_ref, *, tq, lambda_init):
    g = pl.program_id(1)
    qi = pl.program_id(2)
    lane = lax.broadcasted_iota(jnp.int32, (tq, LANE), 1)
    qs = []
    slopes = []
    for r in range(DIFF_REP):
        qh = q_ref[:, r * LANE:(r + 1) * LANE].astype(F32)
        qs.append(jnp.where(lane < DIFF_HEAD_DIM, qh, 0.0).astype(BF16))
        qs.append(jnp.where(lane >= DIFF_HEAD_DIM, qh, 0.0).astype(BF16))
        slopes += [slopes_ref[g * DIFF_REP + r]] * 2

    q0 = N_META + qi * tq
    kcol = lax.broadcasted_iota(jnp.int32, (1, LANE), 1)
    km = km_ref[...]
    s_meta = [jnp.where(lane < N_META, _dot_nt(qs[i], km) + slopes[i] * (kcol - q0).astype(F32), NEG_INF)
              for i in range(4)]
    st = _flash_update(s_meta, _with_ones(vm_ref[...]), _flash_init(4, tq))

    tcol = lax.broadcasted_iota(jnp.int32, (1, tq), 1)

    def tile(j, st, mask):
        start = pl.multiple_of(j * tq, tq)
        k = k_ref[pl.ds(start, tq), :]
        rel = (tcol + (j - qi) * tq).astype(F32)
        s_list = [_dot_nt(qs[i], k) + slopes[i] * rel for i in range(4)]
        if mask is not None:
            s_list = [jnp.where(mask, s, NEG_INF) for s in s_list]
        return _flash_update(s_list, _with_ones(v_ref[pl.ds(start, tq), :]), st)

    st = lax.fori_loop(0, qi, lambda j, st: tile(j, st, None), st)
    st = tile(qi, st, lax.broadcasted_iota(jnp.int32, (tq, tq), 1) <= lax.broadcasted_iota(jnp.int32, (tq, tq), 0))

    lam = _lambda_value(lq1_ref, lk1_ref, lq2_ref, lk2_ref, lambda_init)
    gsub = gsub_ref[...] * (1.0 - lambda_init)
    for r in range(DIFF_REP):
        a0 = st[4 * r + 1]
        a1 = st[4 * r + 3]
        o = a0[:, :LANE] / a0[:, LANE:] - lam * (a1[:, :LANE] / a1[:, LANE:])
        o_ref[:, r * LANE:(r + 1) * LANE] = _rms(o, gsub).astype(BF16)


def _diff_flash(slopes, qd, kdb, vdb, km, vm, gsub, lams, batch, seq, tq, lambda_init):
    nq = seq // tq
    vec = pl.BlockSpec((1, DIFF_HEAD_DIM), lambda b, g, i, s: (0, 0))
    return pl.pallas_call(
        functools.partial(_diff_flash_kernel, tq=tq, lambda_init=lambda_init),
        grid_spec=pltpu.PrefetchScalarGridSpec(
            num_scalar_prefetch=1,
            grid=(batch, DIFF_KV_HEADS, nq),
            in_specs=[
                pl.BlockSpec((tq, 2 * LANE), lambda b, g, i, s: (b * nq + i, g)),
                pl.BlockSpec((seq, LANE), lambda b, g, i, s: (b, g)),
                pl.BlockSpec((seq, LANE), lambda b, g, i, s: (b, g)),
                pl.BlockSpec((LANE, LANE), lambda b, g, i, s: (0, g)),
                pl.BlockSpec((LANE, LANE), lambda b, g, i, s: (0, g)),
                pl.BlockSpec((1, LANE), lambda b, g, i, s: (0, 0)),
                vec, vec, vec, vec,
            ],
            out_specs=pl.BlockSpec((tq, 2 * LANE), lambda b, g, i, s: (b * nq + i, g)),
        ),
        out_shape=jax.ShapeDtypeStruct((batch * seq, DIFF_HEADS * DIFF_V), BF16),
        compiler_params=pltpu.CompilerParams(dimension_semantics=("parallel", "parallel", "parallel"),
                                             vmem_limit_bytes=VMEM_LIMIT),
        name="diff_flash",
    )(slopes, qd, kdb, vdb, km, vm, gsub, *lams)


def _sample_qprep_kernel(qa_ref, gk_ref, wabs_ref, sel_ref, qabs_ref, qrope_ref):
    gk = gk_ref[...]
    for h in range(MLA_HEADS):
        qg = (qa_ref[:, h * LANE:(h + 1) * LANE].astype(F32) * gk).astype(BF16)
        qabs_ref[:, h * LANE:(h + 1) * LANE] = _dot(qg, wabs_ref[h])
        qrope_ref[h] = _dot(qg, sel_ref[...])


def _sample_qprep(qa_s, gk_pad, wabs, sel):
    rows = qa_s.shape[0]
    return pl.pallas_call(
        _sample_qprep_kernel,
        grid=(1,),
        in_specs=[_const_spec(qa_s.shape), _const_spec((1, LANE)), _const_spec(wabs.shape), _const_spec(sel.shape)],
        out_specs=[_const_spec((rows, MLA_HEADS * LANE)), _const_spec((MLA_HEADS, rows, MLA_ROPE))],
        out_shape=[jax.ShapeDtypeStruct((rows, MLA_HEADS * MLA_KV_LORA), F32),
                   jax.ShapeDtypeStruct((MLA_HEADS, rows, MLA_ROPE), F32)],
        name="sample_qprep",
    )(qa_s, gk_pad, wabs, sel)


def _sample_attn_kernel(ptc_ref, ptn_ref, qabs_ref, qrope_ref, qd_ref, latn_ref, kpen_ref, kdn_ref, vdn_ref,
                        wukt_ref, wuvw_ref, slope_ref, qpos_ref, gsub_ref,
                        lq1_ref, lk1_ref, lq2_ref, lk2_ref, hmask_ref, maska_ref, maskd_ref,
                        clat_ref, ckpe_ref, ckd_ref, cvd_ref,
                        oa_ref, od_ref,
                        lat_buf, kpe_buf, kd_buf, vd_buf, sem, lhs_a, qbd_f, qbd, vnew,
                        ma_ref, la_ref, acca_ref, md_ref, ld_ref, accd_ref,
                        *, n_pages, ppc, page, dec_seq, lambda_init):
    seq_id = pl.program_id(0)
    n_seq = pl.num_programs(0)
    n_chunks = n_pages // ppc
    n_arow = MLA_HEADS * dec_seq
    n_up = MLA_HEADS * MLA_NOPE
    grow = 2 * DIFF_REP * dec_seq
    past_len = n_pages * page

    def copies(pt_ref, c, slot):
        out = []
        for p in range(ppc):
            pg = pt_ref[0, 0, c * ppc + p]
            out.append(pltpu.make_async_copy(clat_ref.at[pg], lat_buf.at[slot, pl.ds(p * page, page)],
                                             sem.at[slot, 0]))
            out.append(pltpu.make_async_copy(ckpe_ref.at[pg], kpe_buf.at[slot, p], sem.at[slot, 1]))
            out.append(pltpu.make_async_copy(ckd_ref.at[pg], kd_buf.at[slot, p], sem.at[slot, 2]))
            out.append(pltpu.make_async_copy(cvd_ref.at[pg], vd_buf.at[slot, p], sem.at[slot, 3]))
        return out

    def issue(pt_ref, c, slot):
        for cp in copies(pt_ref, c, slot):
            cp.start()

    @pl.when(seq_id == 0)
    def _():
        issue(ptc_ref, 0, 0)
        issue(ptc_ref, 1, 1)

    lhs_a[0:n_up, :] = wukt_ref[...]
    lhs_a[n_up:, :] = qabs_ref[...].astype(BF16)
    qrope = qrope_ref[...].astype(BF16)
    qbd_f[...] = jnp.zeros(qbd_f.shape, F32)
    lane = lax.broadcasted_iota(jnp.int32, (dec_seq, LANE), 1)
    for g in range(DIFF_KV_HEADS):
        for c in range(2):
            for r in range(DIFF_REP):
                src = qd_ref[:, (g * DIFF_REP + r) * LANE:(g * DIFF_REP + r + 1) * LANE]
                keep = (lane >= DIFF_HEAD_DIM) if c else (lane < DIFF_HEAD_DIM)
                row0 = g * grow + (c * DIFF_REP + r) * dec_seq
                qbd_f[row0:row0 + dec_seq, g * LANE:(g + 1) * LANE] = jnp.where(keep, src, 0.0)
    qbd[...] = qbd_f[...].astype(BF16)
    vnew[...] = jnp.zeros(vnew.shape, F32)
    vnew[0:DIFF_KV_HEADS * dec_seq, :] = vdn_ref[0]
    ma_ref[...] = jnp.full(ma_ref.shape, NEG_INF, F32)
    la_ref[...] = jnp.zeros(la_ref.shape, F32)
    acca_ref[...] = jnp.zeros(acca_ref.shape, F32)
    md_ref[...] = jnp.full(md_ref.shape, NEG_INF, F32)
    ld_ref[...] = jnp.zeros(ld_ref.shape, F32)
    accd_ref[...] = jnp.zeros(accd_ref.shape, F32)
    slope = slope_ref[...]
    qpos = qpos_ref[...]

    def attend(lat, kpe_pages, kd_pages, v_of, kpos0, mask_a, mask_d):
        npg = len(kpe_pages)
        nk = npg * LANE
        latb = lat.astype(BF16)
        r_all = _dot_nt(lhs_a[...], latb)
        kn = r_all[0:n_up]
        kn2 = jnp.sum((kn * kn).reshape(MLA_HEADS, MLA_NOPE, nk), axis=1)
        kpe = kpe_pages[0] if npg == 1 else jnp.concatenate(kpe_pages, axis=1)
        rp2 = jnp.sum(kpe * kpe, axis=0, keepdims=True)
        rinv = lax.rsqrt((kn2 + rp2) * (1.0 / MLA_QK) + EPS)
        s = r_all[n_up:] + _dot(qrope, kpe.astype(BF16))
        s = (s.reshape(dec_seq, MLA_HEADS, nk) * rinv[None]).reshape(n_arow, nk)
        if mask_a is not None:
            s = jnp.where(mask_a > 0.0, s, NEG_INF)
        m_old = ma_ref[...]
        m_new = jnp.maximum(m_old, jnp.max(s, axis=-1, keepdims=True))
        alpha = jnp.exp2(m_old - m_new)
        p = jnp.exp2(s - m_new)
        la_ref[...] = alpha * la_ref[...] + jnp.sum(p, axis=-1, keepdims=True)
        acca_ref[...] = alpha * acca_ref[...] + _dot(p.astype(BF16), latb)
        ma_ref[...] = m_new

        kdb = [kd.astype(BF16) for kd in kd_pages]
        sd = _dot(qbd[...], kdb[0] if npg == 1 else jnp.concatenate(kdb, axis=1))
        kpos = (kpos0 + lax.broadcasted_iota(jnp.int32, (1, nk), 1)).astype(F32)
        sd = sd - slope * (qpos - kpos)
        if mask_d is not None:
            sd = jnp.where(mask_d > 0.0, sd, NEG_INF)
        m_old = md_ref[...]
        m_new = jnp.maximum(m_old, jnp.max(sd, axis=-1, keepdims=True))
        alpha = jnp.exp2(m_old - m_new)
        p = jnp.exp2(sd - m_new)
        ld_ref[...] = alpha * ld_ref[...] + jnp.sum(p, axis=-1, keepdims=True)
        pb = p.astype(BF16)
        for g in range(DIFF_KV_HEADS):
            rows = slice(g * grow, (g + 1) * grow)
            pv = None
            for pi in range(npg):
                t = _dot(pb[rows, pi * LANE:(pi + 1) * LANE], v_of(pi, g).astype(BF16))
                pv = t if pv is None else pv + t
            accd_ref[rows, :] = alpha[rows] * accd_ref[rows, :] + pv
        md_ref[...] = m_new

    def chunk(c, carry):
        t = seq_id * n_chunks + c
        slot = t % N_SLOTS
        ahead = (t + 2) % N_SLOTS

        @pl.when(c + 2 < n_chunks)
        def _():
            issue(ptc_ref, c + 2, ahead)

        @pl.when(jnp.logical_and(c + 2 >= n_chunks, seq_id + 1 < n_seq))
        def _():
            issue(ptn_ref, c + 2 - n_chunks, ahead)

        for cp in copies(ptc_ref, c, slot):
            cp.wait()
        attend(lat_buf[slot],
               [kpe_buf[slot, p] for p in range(ppc)],
               [kd_buf[slot, p] for p in range(ppc)],
               lambda p, g: vd_buf[slot, p, pl.ds(g, page, stride=DIFF_KV_HEADS), :],
               c * (ppc * page), None, None)
        return carry

    lax.fori_loop(0, n_chunks, chunk, 0)

    latn = jnp.concatenate([latn_ref[...], jnp.zeros((LANE - dec_seq, MLA_KV_LORA), F32)], axis=0)
    attend(latn, [kpen_ref[0]], [kdn_ref[0]],
           lambda p, g: vnew[pl.ds(g, LANE, stride=DIFF_KV_HEADS), :],
           past_len, maska_ref[...], maskd_ref[...])

    o_lat = (acca_ref[...] / la_ref[...]).astype(BF16)
    res = _dot(o_lat, wuvw_ref[...]) * hmask_ref[...]
    oa_ref[...] = jnp.sum(res.reshape(dec_seq, MLA_HEADS, MLA_HEADS * MLA_V), axis=1)

    lam = _lambda_value(lq1_ref, lk1_ref, lq2_ref, lk2_ref, lambda_init)
    gsub = gsub_ref[...] * (1.0 - lambda_init)
    for g in range(DIFF_KV_HEADS):
        for r in range(DIFF_REP):
            h = g * DIFF_REP + r
            r0 = g * grow + r * dec_seq
            r1 = r0 + DIFF_REP * dec_seq
            o0 = accd_ref[r0:r0 + dec_seq, :] / ld_ref[r0:r0 + dec_seq, :]
            o1 = accd_ref[r1:r1 + dec_seq, :] / ld_ref[r1:r1 + dec_seq, :]
            od_ref[:, h * LANE:(h + 1) * LANE] = _rms(o0 - lam * o1, gsub)


def _sample_attn(page_table, qabs, qrope, qd_s, lat_s, kpen_t, kdn_t, vdn_r, consts, caches, dec_seq, ppc,
                 lambda_init):
    n_seq, n_pages = page_table.shape
    page = caches[0].shape[1]
    ck = ppc * page
    n_arow = MLA_HEADS * dec_seq
    n_drow = 2 * DIFF_HEADS * dec_seq
    kd_rows = DIFF_KV_HEADS * 2 * DIFF_HEAD_DIM
    pt3 = page_table.reshape(n_seq, 1, n_pages)
    per_seq = lambda w: pl.BlockSpec((dec_seq, w), lambda s: (s, 0))
    per_seq3 = lambda a, b: pl.BlockSpec((1, a, b), lambda s: (s, 0, 0))
    any_spec = pl.BlockSpec(memory_space=pl.ANY)
    in_specs = [
        pl.BlockSpec((1, 1, n_pages), lambda s: (s, 0, 0), memory_space=pltpu.SMEM),
        pl.BlockSpec((1, 1, n_pages), lambda s: (jnp.minimum(s + 1, n_seq - 1), 0, 0), memory_space=pltpu.SMEM),
        pl.BlockSpec((n_arow, MLA_KV_LORA), lambda s: (s, 0)),
        pl.BlockSpec((n_arow, MLA_ROPE), lambda s: (s, 0)),
        per_seq(DIFF_HEADS * DIFF_V), per_seq(MLA_KV_LORA),
        per_seq3(MLA_ROPE, LANE), per_seq3(kd_rows, LANE), per_seq3(DIFF_KV_HEADS * dec_seq, DIFF_V),
    ] + [_const_spec(c.shape) for c in consts] + [any_spec] * 4
    return pl.pallas_call(
        functools.partial(_sample_attn_kernel, n_pages=n_pages, ppc=ppc, page=page, dec_seq=dec_seq,
                          lambda_init=lambda_init),
        grid=(n_seq,),
        in_specs=in_specs,
        out_specs=[per_seq(MLA_HEADS * MLA_V), per_seq(DIFF_HEADS * DIFF_V)],
        out_shape=[jax.ShapeDtypeStruct((n_seq * dec_seq, MLA_HEADS * MLA_V), F32),
                   jax.ShapeDtypeStruct((n_seq * dec_seq, DIFF_HEADS * DIFF_V), F32)],
        scratch_shapes=[
            pltpu.VMEM((N_SLOTS, ck, MLA_KV_LORA), F32), pltpu.VMEM((N_SLOTS, ppc, MLA_ROPE, page), F32),
            pltpu.VMEM((N_SLOTS, ppc, kd_rows, page), F32),
            pltpu.VMEM((N_SLOTS, ppc, DIFF_KV_HEADS * page, DIFF_V), F32),
            pltpu.SemaphoreType.DMA((N_SLOTS, 4)),
            pltpu.VMEM((MLA_HEADS * MLA_NOPE + n_arow, MLA_KV_LORA), BF16),
            pltpu.VMEM((n_drow, kd_rows), F32), pltpu.VMEM((n_drow, kd_rows), BF16),
            pltpu.VMEM((DIFF_KV_HEADS * LANE, DIFF_V), F32),
            pltpu.VMEM((n_arow, 1), F32), pltpu.VMEM((n_arow, 1), F32), pltpu.VMEM((n_arow, MLA_KV_LORA), F32),
            pltpu.VMEM((n_drow, 1), F32), pltpu.VMEM((n_drow, 1), F32), pltpu.VMEM((n_drow, DIFF_V), F32),
        ],
        compiler_params=pltpu.CompilerParams(dimension_semantics=("arbitrary",), vmem_limit_bytes=VMEM_LIMIT),
        name="sample_attn",
    )(pt3, pt3, qabs, qrope, qd_s, lat_s, kpen_t, kdn_t, vdn_r, *consts, *caches)


def _merge_kernel(x_ref, oa_ref, od_ref, sga_ref, sgb_ref, wba_ref, wbb_ref, wo_ref, gffn_ref, wrh_ref, wrl_ref,
                  br_ref, x1_ref, h_ref, gates_ref):
    ya = _dot(oa_ref[...].astype(BF16), wba_ref[...])
    yb = _dot(od_ref[...].astype(BF16), wbb_ref[...])
    m = sga_ref[...].astype(F32) * ya + sgb_ref[...].astype(F32) * yb
    x1 = x_ref[...] + _dot(m.astype(BF16), wo_ref[...])
    x1_ref[...] = x1
    h = _rms(x1, gffn_ref[...])
    hb = h.astype(BF16)
    h_ref[...] = hb
    hl = (h - hb.astype(F32)).astype(BF16)
    logits = _dot(hb, wrh_ref[...]) + _dot(hl, wrh_ref[...]) + _dot(hb, wrl_ref[...]) + br_ref[...]
    iota = lax.broadcasted_iota(jnp.int32, logits.shape, 1).astype(F32)
    vals, hots = [], []
    for _ in range(TOP_K):
        mx = jnp.max(logits, axis=-1, keepdims=True)
        first = jnp.min(jnp.where(logits == mx, iota, float(N_EXPERTS)), axis=-1, keepdims=True)
        hot = iota == first
        vals.append(mx)
        hots.append(hot)
        logits = jnp.where(hot, NEG_INF, logits)
    es = [jnp.exp(v - vals[0]) for v in vals]
    den = es[0] + es[1] + es[2] + es[3]
    gates = jnp.zeros(logits.shape, F32)
    for e, hot in zip(es, hots):
        gates = jnp.where(hot, e / den, gates)
    gates_ref[...] = gates


def _merge(x, oa, od, sga, sgb, wts, tm):
    rows = x.shape[0]
    row = lambda i: (i, 0)
    in_specs = [pl.BlockSpec((tm, D_MODEL), row), pl.BlockSpec((tm, MLA_HEADS * MLA_V), row),
                pl.BlockSpec((tm, DIFF_HEADS * DIFF_V), row), pl.BlockSpec((tm, D_MODEL), row),
                pl.BlockSpec((tm, D_MODEL), row)] + [_const_spec(w.shape) for w in wts]
    return pl.pallas_call(
        _merge_kernel,
        grid=(rows // tm,),
        in_specs=in_specs,
        out_specs=[pl.BlockSpec((tm, D_MODEL), row), pl.BlockSpec((tm, D_MODEL), row),
                   pl.BlockSpec((tm, N_EXPERTS), row)],
        out_shape=[jax.ShapeDtypeStruct((rows, D_MODEL), F32), jax.ShapeDtypeStruct((rows, D_MODEL), BF16),
                   jax.ShapeDtypeStruct((rows, N_EXPERTS), F32)],
        compiler_params=pltpu.CompilerParams(dimension_semantics=("parallel",), vmem_limit_bytes=VMEM_LIMIT),
        name="merge_router",
    )(x, oa, od, sga, sgb, *wts)


def _split_gate_up_kernel(w_ref, perm_ref, wg_ref, wu_ref):
    perm = perm_ref[...]
    for b in range(2 * D_FF // MXU):
        t = _dot(w_ref[0, :, b * MXU:(b + 1) * MXU].astype(BF16), perm)
        wg_ref[0, :, b * LANE:(b + 1) * LANE] = t[:, :LANE].astype(BF16)
        wu_ref[0, :, b * LANE:(b + 1) * LANE] = t[:, LANE:].astype(BF16)


def _split_gate_up(w_gate_up, tk):
    n_e, d_in, d_gu = w_gate_up.shape
    j = jnp.arange(MXU)
    src = jnp.where(j < LANE, 2 * j, 2 * (j - LANE) + 1)
    perm = (jnp.arange(MXU)[:, None] == src[None, :]).astype(BF16)
    blk = lambda e, k: (e, k, 0)
    return pl.pallas_call(
        _split_gate_up_kernel,
        grid=(n_e, d_in // tk),
        in_specs=[pl.BlockSpec((1, tk, d_gu), blk), _const_spec((MXU, MXU))],
        out_specs=[pl.BlockSpec((1, tk, d_gu // 2), blk), pl.BlockSpec((1, tk, d_gu // 2), blk)],
        out_shape=[jax.ShapeDtypeStruct((n_e, d_in, d_gu // 2), BF16)] * 2,
        compiler_params=pltpu.CompilerParams(dimension_semantics=("parallel", "parallel"),
                                             vmem_limit_bytes=VMEM_LIMIT),
        name="split_gate_up",
    )(w_gate_up, perm)


def _moe_kernel(cnt_ref, h_ref, gt_ref, rr_ref, x1_ref, wg_ref, wu_ref, bg_ref, bu_ref, wd_ref, bd_ref,
                o_ref, *, sub):
    b = pl.program_id(0)
    e = pl.program_id(1)
    blk = h_ref.shape[0]

    @pl.when(e == 0)
    def _():
        o_ref[...] = x1_ref[...]

    count = cnt_ref[b * N_EXPERTS + e]
    rank_row = rr_ref[pl.ds(e, 1), :]
    gate_row = gt_ref[pl.ds(e, 1), :]
    row_id = lax.broadcasted_iota(jnp.int32, (sub, blk), 0).astype(F32)
    h = h_ref[...]

    def tile(i, carry):
        base = (i * sub).astype(F32)
        hit = rank_row - base == row_id
        pick = jnp.where(hit, 1.0, 0.0).astype(BF16)
        gate = jnp.sum(jnp.where(hit, gate_row, 0.0), axis=-1, keepdims=True)
        xs = _dot(pick, h).astype(BF16)
        glu = jnp.minimum(_dot(xs, wg_ref[0]) + bg_ref[0], SWIGLU_LIMIT)
        lin = jnp.clip(_dot(xs, wu_ref[0]) + bu_ref[0], -SWIGLU_LIMIT, SWIGLU_LIMIT)
        act = glu * jax.nn.sigmoid(SWIGLU_ALPHA * glu) * (lin + 1.0)
        y = _dot(act.astype(BF16), wd_ref[0]) + bd_ref[0]
        o_ref[...] += lax.dot_general(pick, (gate * y).astype(BF16), (((0,), (0,)), ((), ())),
                                      preferred_element_type=F32)
        return carry

    lax.fori_loop(0, (count + sub - 1) // sub, tile, 0)


def _moe(h, gates, x1, wg, wu, bg, bu, wd, bd, blk):
    rows = h.shape[0]
    n_blk = rows // blk
    sub = min(3 * blk // 16, blk) if blk >= MXU else blk
    gates_t = gates.T
    r3 = (gates_t > 0.0).reshape(N_EXPERTS, n_blk, blk)
    upper = (jnp.arange(blk)[:, None] <= jnp.arange(blk)[None, :]).astype(BF16)
    running = jnp.einsum("ebs,st->ebt", r3.astype(BF16), upper, preferred_element_type=F32)
    rank_t = jnp.where(r3, running - 1.0, -1.0).reshape(N_EXPERTS, rows)
    counts = running[:, :, -1].astype(jnp.int32).T.reshape(n_blk * N_EXPERTS)
    tok = lambda i, e, c: (i, 0)
    lanes = lambda i, e, c: (0, i)
    exp3 = lambda i, e, c: (e, 0, 0)
    return pl.pallas_call(
        functools.partial(_moe_kernel, sub=sub),
        grid_spec=pltpu.PrefetchScalarGridSpec(
            num_scalar_prefetch=1,
            grid=(n_blk, N_EXPERTS),
            in_specs=[pl.BlockSpec((blk, D_MODEL), tok), pl.BlockSpec((N_EXPERTS, blk), lanes),
                      pl.BlockSpec((N_EXPERTS, blk), lanes),
                      pl.BlockSpec((blk, D_MODEL), tok),
                      pl.BlockSpec((1, D_MODEL, D_FF), exp3), pl.BlockSpec((1, D_MODEL, D_FF), exp3),
                      pl.BlockSpec((1, 1, D_FF), exp3), pl.BlockSpec((1, 1, D_FF), exp3),
                      pl.BlockSpec((1, D_FF, D_MODEL), exp3), pl.BlockSpec((1, 1, D_MODEL), exp3)],
            out_specs=pl.BlockSpec((blk, D_MODEL), tok),
        ),
        out_shape=jax.ShapeDtypeStruct((rows, D_MODEL), F32),
        compiler_params=pltpu.CompilerParams(dimension_semantics=("parallel", "arbitrary"),
                                             vmem_limit_bytes=VMEM_LIMIT),
        name="moe",
    )(counts, h, gates_t, rank_t, x1, wg, wu, bg, bu, wd, bd)


def _rope_tables(pos):
    inv = 1.0 / (ROPE_BASE ** (jnp.arange(0, MLA_ROPE, 2, dtype=F32) / MLA_ROPE))
    ang = pos.astype(F32)[:, None] * inv[None, :]
    cos, sin = jnp.cos(ang), jnp.sin(ang)
    n = pos.shape[0]
    cq = jnp.concatenate([jnp.ones((n, MLA_NOPE), F32), cos, cos, jnp.zeros((n, LANE - MLA_QK), F32)], axis=1)
    sq = jnp.concatenate([jnp.zeros((n, MLA_NOPE), F32), -sin, sin, jnp.zeros((n, LANE - MLA_QK), F32)], axis=1)
    return cq, sq


def _pad_lanes(w, width=LANE):
    return jnp.pad(w, [(0, 0)] * (w.ndim - 1) + [(0, width - w.shape[-1])])


def _pick_tile(n, cap):
    t = cap
    while n % t:
        t //= 2
    return t


def kernel(x_prompt, x_sample, cache_mla_latent, cache_mla_krope, cache_diff_k, cache_diff_v, page_table, meta_tokens, g_attn, w_in, g_q_lat, w_uq, g_kv_lat, w_uk, w_uv, g_mla_q, g_mla_k, g_diff_q, g_diff_k, lambda_q1, lambda_k1, lambda_q2, lambda_k2, g_subln, w_branch_a, w_branch_b, w_o, g_ffn, w_router, b_router, w_gate_up, b_gate_up, w_down, b_down):
    depth = w_in.shape[0]
    assert depth == 1, "single-layer trunk"
    l = 0
    batch, seq, _ = x_prompt.shape
    n_seq, dec_seq, _ = x_sample.shape
    n_pages = page_table.shape[1]
    n_pool, page = cache_mla_latent.shape[1:3]
    assert page == LANE and dec_seq == 8
    past_len = n_pages * page
    lambda_init = 0.8 - 0.6 * math.exp(-0.3 * l)

    w = w_in[l]
    o = 0
    parts = []
    for width in (MLA_Q_LORA, MLA_KV_LORA, MLA_ROPE, DIFF_HEADS * 2 * DIFF_HEAD_DIM,
                  DIFF_KV_HEADS * 2 * DIFF_HEAD_DIM, DIFF_KV_HEADS * DIFF_V, D_MODEL, D_MODEL):
        parts.append(w[:, o:o + width])
        o += width
    w_ql, w_kvl, w_kpe, w_qd, w_kd, w_vd, w_ga, w_gb = parts
    half = MLA_ROPE // 2
    w_kpe_sw = jnp.concatenate([w_kpe[:, half:], w_kpe[:, :half]], axis=1)
    place = lambda m: jnp.pad(m, ((0, 0), (MLA_NOPE, LANE - MLA_QK)))
    wa = jnp.concatenate([w_ql, w_kvl, place(w_kpe), place(w_kpe_sw)], axis=1).astype(BF16)
    uq = w_uq[l]
    uq_sw = jnp.concatenate([jnp.zeros_like(uq[..., :MLA_NOPE]), uq[..., MLA_NOPE + half:],
                             uq[..., MLA_NOPE:MLA_NOPE + half]], axis=-1)
    wq = _pad_lanes(uq).reshape(MLA_Q_LORA, MLA_HEADS * LANE).astype(BF16)
    wqs = _pad_lanes(uq_sw).reshape(MLA_Q_LORA, MLA_HEADS * LANE).astype(BF16)
    wuk = _pad_lanes(w_uk[l]).reshape(MLA_KV_LORA, MLA_HEADS * LANE).astype(BF16)
    wuv = w_uv[l].reshape(MLA_KV_LORA, MLA_HEADS * MLA_V).astype(BF16)
    gq = _pad_lanes(g_mla_q[l] * (MLA_SCALE * LOG2E))[None]
    gk = _pad_lanes(g_mla_k[l])[None]
    gdq = (jnp.concatenate([g_diff_q[l]] * 2) * (DIFF_SCALE * LOG2E))[None]
    gdk = jnp.concatenate([g_diff_k[l]] * 2)[None]
    grp = jnp.arange(LANE) // DIFF_HEAD_DIM
    b2 = (grp[:, None] == grp[None, :]).astype(BF16)
    in_wts = (g_attn[l][None], wa, g_q_lat[l][None], wq, wqs, gq, g_kv_lat[l][None], wuk, gk, wuv,
              w_qd.astype(BF16), gdq, w_kd.astype(BF16), gdk, w_vd.astype(BF16), w_ga.astype(BF16),
              w_gb.astype(BF16), b2)

    tm_p = _pick_tile(seq, 256)
    cq_p, sq_p = _rope_tables(N_META + jnp.arange(seq, dtype=jnp.int32))
    outs_p = _inproj(x_prompt.reshape(batch * seq, D_MODEL), cq_p, sq_p, in_wts, tm_p)
    (qa_p, lat_p, kpe_p, kmla_p, vmla_p, qd_p, kd_p, kdb_p, vd_p, vdb_p, sga_p, sgb_p) = outs_p

    n_s = n_seq * dec_seq
    tm_s = _pick_tile(n_s, 256)
    n_small = -(-(n_s + N_META) // tm_s) * tm_s
    x_small = jnp.concatenate([x_sample.reshape(n_s, D_MODEL), meta_tokens.astype(F32),
                               jnp.zeros((n_small - n_s - N_META, D_MODEL), F32)], axis=0)
    pos_small = jnp.concatenate([jnp.tile(past_len + jnp.arange(dec_seq, dtype=jnp.int32), n_seq),
                                 jnp.arange(N_META, dtype=jnp.int32),
                                 jnp.zeros((n_small - n_s - N_META,), jnp.int32)])
    cq_s, sq_s = _rope_tables(pos_small)
    outs_s = _inproj(x_small, cq_s, sq_s, in_wts, tm_s)
    (qa_s, lat_s, kpe_s, kmla_s, vmla_s, qd_s, kd_s, kdb_s, vd_s, vdb_s, sga_s, sgb_s) = outs_s
    meta = slice(n_s, n_s + N_META)

    tq = _pick_tile(seq, 1024)
    pad_keys = lambda a: jnp.pad(a[meta], ((0, LANE - N_META), (0, 0)))
    o_a_p = _mla_flash(qa_p, kmla_p, vmla_p, pad_keys(kmla_s), pad_keys(vmla_s), batch, seq, tq)
    slopes = jnp.exp2(-8.0 * jnp.arange(1, DIFF_HEADS + 1, dtype=F32) / DIFF_HEADS) * LOG2E
    lams = (lambda_q1[l][None], lambda_k1[l][None], lambda_q2[l][None], lambda_k2[l][None])
    gsub = g_subln[l][None]
    o_d_p = _diff_flash(slopes, qd_p, kdb_p, vdb_p, pad_keys(kdb_s), pad_keys(vdb_s), gsub, lams, batch, seq, tq,
                        lambda_init)

    ukt = jnp.transpose(w_uk[l], (1, 2, 0))
    wabs = jnp.pad(ukt, ((0, 0), (0, LANE - MLA_NOPE), (0, 0))).astype(BF16)
    sel = (jnp.arange(LANE)[:, None] == (MLA_NOPE + jnp.arange(MLA_ROPE))[None, :]).astype(BF16)
    qabs, qrope = _sample_qprep(qa_s[:n_s], gk, wabs, sel)
    n_arow = MLA_HEADS * dec_seq
    qabs = qabs.reshape(n_s * MLA_HEADS, MLA_KV_LORA)
    qrope = jnp.transpose(qrope, (1, 0, 2)).reshape(n_s * MLA_HEADS, MLA_ROPE)
    drow = jnp.arange(2 * DIFF_HEADS * dec_seq)
    d_head = (drow // (2 * DIFF_REP * dec_seq)) * DIFF_REP + (drow // dec_seq) % DIFF_REP
    slope_rows = slopes[d_head][:, None]
    qpos_rows = (past_len + drow % dec_seq).astype(F32)[:, None]
    arow = jnp.arange(n_arow)
    hmask = (arow[:, None] % MLA_HEADS == jnp.arange(MLA_HEADS * MLA_V)[None, :] // MLA_V).astype(F32)
    new_key = jnp.arange(LANE)[None, :]
    mask_a = (new_key <= (arow // MLA_HEADS)[:, None]).astype(F32)
    mask_d = (new_key <= (drow % dec_seq)[:, None]).astype(F32)
    consts = (ukt.reshape(MLA_HEADS * MLA_NOPE, MLA_KV_LORA).astype(BF16), wuv, slope_rows, qpos_rows,
              gsub) + lams + (hmask, mask_a, mask_d)
    kd_rows = DIFF_KV_HEADS * 2 * DIFF_HEAD_DIM
    caches = (cache_mla_latent[l],
              jnp.transpose(cache_mla_krope[l], (0, 2, 1)),
              jnp.transpose(cache_diff_k[l], (0, 2, 3, 4, 1)).reshape(n_pool, kd_rows, page),
              cache_diff_v[l].reshape(n_pool, page * DIFF_KV_HEADS, DIFF_V))
    pad_new = lambda a: jnp.pad(a, ((0, 0), (0, 0), (0, LANE - dec_seq)))
    kpen_t = pad_new(jnp.transpose(kpe_s[:n_s].reshape(n_seq, dec_seq, MLA_ROPE), (0, 2, 1)))
    kdn_t = pad_new(jnp.transpose(kd_s[:n_s].reshape(n_seq, dec_seq, kd_rows), (0, 2, 1)))
    vdn_r = vd_s[:n_s].reshape(n_seq, dec_seq * DIFF_KV_HEADS, DIFF_V)
    ppc = _pick_tile(n_pages // 2, 16)
    o_a_s, o_d_s = _sample_attn(page_table, qabs, qrope, qd_s[:n_s].astype(F32), lat_s[:n_s], kpen_t, kdn_t,
                                vdn_r, consts, caches, dec_seq, ppc, lambda_init)

    wr = w_router[l]
    wr_hi = wr.astype(BF16)
    wr_lo = (wr - wr_hi.astype(F32)).astype(BF16)
    merge_wts = (w_branch_a[l].astype(BF16), w_branch_b[l].astype(BF16), w_o[l].astype(BF16), g_ffn[l][None],
                 wr_hi, wr_lo, b_router[l][None].astype(F32))
    wg, wu = _split_gate_up(w_gate_up[l], 512)
    bg = b_gate_up[l][:, None, 0::2]
    bu = b_gate_up[l][:, None, 1::2]
    wd = w_down[l].astype(BF16)
    bd = b_down[l][:, None, :]

    def channel(x, oa, od, sga, sgb, cap):
        x1, h, gates = _merge(x, oa, od, sga, sgb, merge_wts, _pick_tile(x.shape[0], 256))
        return _moe(h, gates, x1, wg, wu, bg, bu, wd, bd, _pick_tile(x.shape[0], cap))

    y_p = channel(x_prompt.reshape(batch * seq, D_MODEL), o_a_p, o_d_p, sga_p, sgb_p, 1024)
    y_s = channel(x_sample.reshape(n_s, D_MODEL), o_a_s, o_d_s, sga_s[:n_s], sgb_s[:n_s], 1024)

    def prompt_rows(real, small, tail):
        real = real.reshape((batch, seq) + tail)
        m = jnp.broadcast_to(small[meta].reshape((1, N_META) + tail), (batch, N_META) + tail)
        return jnp.concatenate([m, real], axis=1)[None]

    def sample_rows(small, tail):
        return small[:n_s].reshape((1, n_seq, dec_seq) + tail)

    kshape = (DIFF_KV_HEADS, 2, DIFF_HEAD_DIM)
    vshape = (DIFF_KV_HEADS, DIFF_V)
    return (y_p.reshape(batch, seq, D_MODEL), y_s.reshape(n_seq, dec_seq, D_MODEL),
            prompt_rows(lat_p, lat_s, (MLA_KV_LORA,)), prompt_rows(kpe_p, kpe_s, (MLA_ROPE,)),
            prompt_rows(kd_p, kd_s, kshape), prompt_rows(vd_p, vd_s, vshape),
            sample_rows(lat_s, (MLA_KV_LORA,)), sample_rows(kpe_s, (MLA_ROPE,)),
            sample_rows(kd_s, kshape), sample_rows(vd_s, vshape))
```
